```python
import jax, jax.numpy as jnp
from jax import lax
import numpy as np

D_MODEL = 1024
BATCH = 8
SEQ = 2048
DEPTH = 1
DEC_BATCH = 128
DEC_SEQ = 1
PAST_LEN = 16384
PAGE_SIZE = 128

CHUNK = 128
W_A = D_MODEL
A_GROUPS = 8
A_GROUP_DIM = W_A // A_GROUPS
W_B = D_MODEL
B_HEADS = 16
B_HEAD_DIM = W_B // B_HEADS
CONV_WIDTH = 4
LRU_C = 8.0
D_FF = int(round(8 * D_MODEL / 3 / 64)) * 64
EPS = 1e-6
SPLITS = (W_A, 2 * W_A, 2 * W_A + W_B, 2 * W_A + 2 * W_B, 2 * W_A + 2 * W_B + D_MODEL)
W_IN_COLS = 2 * W_A + 2 * W_B + 2 * D_MODEL

kernel_name = "macaron_gmlp_rglru_gated_hybrid_step"


def rms_norm(x, g):
    xf = x.astype(jnp.float32)
    y = xf * lax.rsqrt(jnp.mean(xf * xf, axis=-1, keepdims=True) + EPS)
    return (y * g.astype(jnp.float32)).astype(x.dtype)


def swiglu(x, w_gate, w_up, w_down):
    return (jax.nn.silu(x @ w_gate) * (x @ w_up)) @ w_down


def chunk_spatial_gate(u, v, w_s, b_s):
    B, T, _ = v.shape
    L = CHUNK if T >= CHUNK else T
    pad = (-T) % L
    vp = jnp.pad(v, ((0, 0), (0, pad), (0, 0)))
    n = (T + pad) // L
    vc = vp.reshape(B, n, L, A_GROUPS, A_GROUP_DIM)
    mask = jnp.tril(jnp.ones((L, L), dtype=bool))
    w = jnp.where(mask, w_s[:, :L, :L], 0).astype(v.dtype)
    s = jnp.einsum('gts,bnsgd->bntgd', w, vc) + b_s[:, :L].T[None, None, :, :, None].astype(v.dtype)
    s = s.reshape(B, n * L, W_A)[:, :T]
    return u * s


def causal_conv(xb, conv_prev, w, b):
    T = xb.shape[1]
    xp = jnp.concatenate([conv_prev.astype(xb.dtype), xb], axis=1)
    y = b.astype(xb.dtype)
    for k in range(CONV_WIDTH):
        y = y + w[k] * xp[:, k:k + T]
    return y, xp[:, T:]


def rg_lru(x, h0, w_r, b_r, w_i, b_i, lam):
    B, T, _ = x.shape
    xf = x.astype(jnp.float32)
    xh = xf.reshape(B, T, B_HEADS, B_HEAD_DIM)
    r = jax.nn.sigmoid(jnp.einsum('bthi,hij->bthj', xh, w_r.astype(jnp.float32)) + b_r.astype(jnp.float32)).reshape(B, T, W_B)
    i = jax.nn.sigmoid(jnp.einsum('bthi,hij->bthj', xh, w_i.astype(jnp.float32)) + b_i.astype(jnp.float32)).reshape(B, T, W_B)
    log_a = -LRU_C * r * jax.nn.softplus(-lam.astype(jnp.float32))
    a = jnp.exp(log_a)
    b_in = jnp.sqrt(-jnp.expm1(2.0 * log_a)) * (i * xf)

    def combine(left, right):
        a1, b1 = left
        a2, b2 = right
        return a1 * a2, a2 * b1 + b2

    a_cum, b_cum = lax.associative_scan(combine, (a, b_in), axis=1)
    h = a_cum * h0.astype(jnp.float32)[:, None, :] + b_cum
    return h.astype(x.dtype), h[:, -1].astype(x.dtype)


def decoder_layer(x, h0, conv_prev,
                  ffn1_norm, ffn1_w_gate, ffn1_w_up, ffn1_w_down,
                  mix_norm, w_in, gmlp_v_norm, spatial_w, spatial_b,
                  conv_w, conv_b, lru_w_r, lru_b_r, lru_w_i, lru_b_i, lru_lambda,
                  proj_a, proj_b, w_out,
                  ffn2_norm, ffn2_w_gate, ffn2_w_up, ffn2_w_down):
    x = x + 0.5 * swiglu(rms_norm(x, ffn1_norm), ffn1_w_gate, ffn1_w_up, ffn1_w_down)
    h = rms_norm(x, mix_norm)
    z = h @ w_in
    u, v, xb, gate_br, g_a, g_b = jnp.split(z, SPLITS, axis=-1)
    u = jax.nn.gelu(u)
    v = rms_norm(jax.nn.gelu(v), gmlp_v_norm)
    y_a = chunk_spatial_gate(u, v, spatial_w, spatial_b)
    xc, conv_new = causal_conv(xb, conv_prev, conv_w, conv_b)
    hl, h_last = rg_lru(xc, h0, lru_w_r, lru_b_r, lru_w_i, lru_b_i, lru_lambda)
    y_b = hl * jax.nn.gelu(gate_br)
    m = jax.nn.sigmoid(g_a) * (y_a @ proj_a) + jax.nn.sigmoid(g_b) * (y_b @ proj_b)
    x = x + m @ w_out
    x = x + 0.5 * swiglu(rms_norm(x, ffn2_norm), ffn2_w_gate, ffn2_w_up, ffn2_w_down)
    return x, h_last, conv_new, v


def setup_inputs(seed: int = 0) -> dict:
    key = jax.random.key(seed)
    ks = jax.random.split(key, 32)
    f32 = jnp.float32

    def nrm(k, shape, scale):
        return jax.random.normal(k, shape, f32) * scale

    def gain(k, shape):
        return 1.0 + 0.01 * jax.random.normal(k, shape, f32)

    u_a = jax.random.uniform(ks[20], (DEPTH, W_B), f32, 0.9, 0.999)
    a0 = u_a ** (1.0 / LRU_C)
    lam = jnp.log(a0) - jnp.log1p(-a0)

    return {
        "x_prompt": nrm(ks[0], (BATCH, SEQ, D_MODEL), 1.0),
        "x_sample": nrm(ks[1], (DEC_BATCH, DEC_SEQ, D_MODEL), 1.0),
        "state_lru_h": nrm(ks[2], (DEPTH, DEC_BATCH, W_B), 0.5),
        "state_conv": nrm(ks[3], (DEPTH, DEC_BATCH, CONV_WIDTH - 1, W_B), 1.0),
        "ffn1_norm": gain(ks[4], (DEPTH, D_MODEL)),
        "ffn1_w_gate": nrm(ks[5], (DEPTH, D_MODEL, D_FF), D_MODEL ** -0.5),
        "ffn1_w_up": nrm(ks[6], (DEPTH, D_MODEL, D_FF), D_MODEL ** -0.5),
        "ffn1_w_down": nrm(ks[7], (DEPTH, D_FF, D_MODEL), D_FF ** -0.5),
        "mix_norm": gain(ks[8], (DEPTH, D_MODEL)),
        "w_in": nrm(ks[9], (DEPTH, D_MODEL, W_IN_COLS), D_MODEL ** -0.5),
        "gmlp_v_norm": gain(ks[10], (DEPTH, W_A)),
        "spatial_w": nrm(ks[11], (DEPTH, A_GROUPS, CHUNK, CHUNK), CHUNK ** -0.5),
        "spatial_b": 1.0 + 0.1 * jax.random.normal(ks[12], (DEPTH, A_GROUPS, CHUNK), f32),
        "conv_w": nrm(ks[13], (DEPTH, CONV_WIDTH, W_B), CONV_WIDTH ** -0.5),
        "conv_b": nrm(ks[14], (DEPTH, W_B), 0.01),
        "lru_w_r": nrm(ks[15], (DEPTH, B_HEADS, B_HEAD_DIM, B_HEAD_DIM), B_HEAD_DIM ** -0.5),
        "lru_b_r": nrm(ks[16], (DEPTH, B_HEADS, B_HEAD_DIM), 0.01),
        "lru_w_i": nrm(ks[17], (DEPTH, B_HEADS, B_HEAD_DIM, B_HEAD_DIM), B_HEAD_DIM ** -0.5),
        "lru_b_i": nrm(ks[18], (DEPTH, B_HEADS, B_HEAD_DIM), 0.01),
        "lru_lambda": lam,
        "proj_a": nrm(ks[21], (DEPTH, W_A, D_MODEL), W_A ** -0.5),
        "proj_b": nrm(ks[22], (DEPTH, W_B, D_MODEL), W_B ** -0.5),
        "w_out": nrm(ks[23], (DEPTH, D_MODEL, D_MODEL), D_MODEL ** -0.5),
        "ffn2_norm": gain(ks[24], (DEPTH, D_MODEL)),
        "ffn2_w_gate": nrm(ks[25], (DEPTH, D_MODEL, D_FF), D_MODEL ** -0.5),
        "ffn2_w_up": nrm(ks[26], (DEPTH, D_MODEL, D_FF), D_MODEL ** -0.5),
        "ffn2_w_down": nrm(ks[27], (DEPTH, D_FF, D_MODEL), D_FF ** -0.5),
        "final_norm": gain(ks[28], (D_MODEL,)),
    }


def reference(x_prompt, x_sample, state_lru_h, state_conv,
              ffn1_norm, ffn1_w_gate, ffn1_w_up, ffn1_w_down,
              mix_norm, w_in, gmlp_v_norm, spatial_w, spatial_b,
              conv_w, conv_b, lru_w_r, lru_b_r, lru_w_i, lru_b_i, lru_lambda,
              proj_a, proj_b, w_out,
              ffn2_norm, ffn2_w_gate, ffn2_w_up, ffn2_w_down, final_norm):
    xp = x_prompt
    xs = x_sample
    h_p_list, c_p_list, h_s_list, c_s_list, v_s_list = [], [], [], [], []
    for l in range(DEPTH):
        weights = (ffn1_norm[l], ffn1_w_gate[l], ffn1_w_up[l], ffn1_w_down[l],
                   mix_norm[l], w_in[l], gmlp_v_norm[l], spatial_w[l], spatial_b[l],
                   conv_w[l], conv_b[l], lru_w_r[l], lru_b_r[l], lru_w_i[l], lru_b_i[l], lru_lambda[l],
                   proj_a[l], proj_b[l], w_out[l],
                   ffn2_norm[l], ffn2_w_gate[l], ffn2_w_up[l], ffn2_w_down[l])
        h0_p = jnp.zeros((xp.shape[0], W_B), xp.dtype)
        c0_p = jnp.zeros((xp.shape[0], CONV_WIDTH - 1, W_B), xp.dtype)
        xp, h_p, c_p, _ = decoder_layer(xp, h0_p, c0_p, *weights)
        xs, h_s, c_s, v_s = decoder_layer(xs, state_lru_h[l], state_conv[l], *weights)
        h_p_list.append(h_p)
        c_p_list.append(c_p)
        h_s_list.append(h_s)
        c_s_list.append(c_s)
        v_s_list.append(v_s)
    y_prompt = rms_norm(xp, final_norm)
    y_sample = rms_norm(xs, final_norm)
    new_lru_h_prompt = jnp.stack(h_p_list, axis=0)
    new_conv_prompt = jnp.stack(c_p_list, axis=0)
    new_lru_h_sample = jnp.stack(h_s_list, axis=0)
    new_conv_sample = jnp.stack(c_s_list, axis=0)
    new_chunk_v_sample = jnp.stack(v_s_list, axis=0)
    return (y_prompt, y_sample, new_lru_h_prompt, new_conv_prompt, new_lru_h_sample, new_conv_sample, new_chunk_v_sample)
```

```python
import functools

import jax
import jax.numpy as jnp
from jax import lax
from jax.experimental import pallas as pl
from jax.experimental.pallas import tpu as pltpu

D_MODEL = 1024
CHUNK = 128
A_GROUPS = 8
A_GROUP_DIM = D_MODEL // A_GROUPS
B_HEADS = 16
B_HEAD_DIM = D_MODEL // B_HEADS
CONV_WIDTH = 4
LRU_C = 8.0
EPS = 1e-6

SUBLANES = 8
MXU_DIM = 256
GATE_BLOCKS = D_MODEL // MXU_DIM
HEADS_PER_BLOCK = MXU_DIM // B_HEAD_DIM

FFN_TILE = 256
MIX_TILE = 256
HIST = SUBLANES

F32 = jnp.float32
BF16 = jnp.bfloat16


def _vmem_limit(weight_bytes, tile_rows, live_f32_cols):
    act = tile_rows * live_f32_cols * 4
    return int(1.25 * (weight_bytes + act)) + (4 << 20)


def _rms(x, g):
    return x * lax.rsqrt(jnp.mean(x * x, axis=-1, keepdims=True) + EPS) * g


def _resident(shape):
    nd = len(shape)
    return pl.BlockSpec(shape, lambda *_: (0,) * nd, pipeline_mode=pl.Buffered(1))


def _ffn_kernel(x_ref, g_ref, wg_ref, wu_ref, wd_ref, fg_ref, o_ref, *, final_norm):
    x = x_ref[...]
    h = _rms(x, g_ref[...]).astype(BF16)
    gate = jnp.dot(h, wg_ref[...], preferred_element_type=F32)
    up = jnp.dot(h, wu_ref[...], preferred_element_type=F32)
    act = (jax.nn.silu(gate) * up).astype(BF16)
    y = x + 0.5 * jnp.dot(act, wd_ref[...], preferred_element_type=F32)
    if final_norm:
        y = _rms(y, fg_ref[...])
    o_ref[...] = y


def _ffn(x2d, g, wg, wu, wd, fg, *, final_norm):
    m, d = x2d.shape
    dff = wg.shape[1]
    tm = min(FFN_TILE, m)
    row = pl.BlockSpec((tm, d), lambda i: (i, 0))
    weight_bytes = 2 * (wg.size + wu.size + wd.size)
    return pl.pallas_call(
        functools.partial(_ffn_kernel, final_norm=final_norm),
        grid=(m // tm,),
        in_specs=[row, _resident(g.shape), _resident(wg.shape), _resident(wu.shape),
                  _resident(wd.shape), _resident(fg.shape)],
        out_specs=row,
        out_shape=jax.ShapeDtypeStruct((m, d), F32),
        compiler_params=pltpu.CompilerParams(
            dimension_semantics=("arbitrary",),
            vmem_limit_bytes=_vmem_limit(weight_bytes, tm, 4 * d + 4 * dff)),
    )(x2d, g, wg, wu, wd, fg)


def _in_proj(h, win_ref, k):
    return jnp.dot(h, win_ref[:, k * D_MODEL:(k + 1) * D_MODEL], preferred_element_type=F32)


def _lru_coeffs(xc, wr_ref, br_ref, wi_ref, bi_ref, lam_ref):
    xcb = xc.astype(BF16)
    r_parts, i_parts = [], []
    for j in range(GATE_BLOCKS):
        blk = xcb[:, j * MXU_DIM:(j + 1) * MXU_DIM]
        r_parts.append(jnp.dot(blk, wr_ref[j], preferred_element_type=F32))
        i_parts.append(jnp.dot(blk, wi_ref[j], preferred_element_type=F32))
    r = jax.nn.sigmoid(jnp.concatenate(r_parts, axis=-1) + br_ref[...])
    i = jax.nn.sigmoid(jnp.concatenate(i_parts, axis=-1) + bi_ref[...])
    lam = lam_ref[...]
    softplus_neg_lam = jnp.maximum(-lam, 0.0) + jnp.log1p(jnp.exp(-jnp.abs(lam)))
    log_a = -LRU_C * r * softplus_neg_lam
    a = jnp.exp(log_a)
    th = jnp.tanh(log_a)
    b = jnp.sqrt(-2.0 * th / (1.0 - th)) * (i * xc)
    return a, b


def _merge(x, ya_b16, yb, g_a, g_b, pa_ref, pb_ref, wo_ref):
    m = (jax.nn.sigmoid(g_a) * jnp.dot(ya_b16, pa_ref[...], preferred_element_type=F32)
         + jax.nn.sigmoid(g_b) * jnp.dot(yb.astype(BF16), pb_ref[...], preferred_element_type=F32))
    return x + jnp.dot(m.astype(BF16), wo_ref[...], preferred_element_type=F32)


def _mixer_prompt_kernel(x_ref, mixg_ref, win_ref, vng_ref, ws_ref, bs_ref, cw_ref, cb_ref,
                         wr_ref, br_ref, wi_ref, bi_ref, lam_ref, pa_ref, pb_ref, wo_ref,
                         xo_ref, hlast_ref, convnew_ref,
                         xb_scr, a_scr, b_scr, ya_scr, h_scr):
    tm = x_ref.shape[0]

    @pl.when(pl.program_id(1) == 0)
    def _():
        xb_scr[0:HIST, :] = jnp.zeros((HIST, D_MODEL), F32)
        h_scr[...] = jnp.zeros((1, D_MODEL), F32)

    x = x_ref[...]
    h = _rms(x, mixg_ref[...]).astype(BF16)

    u = jax.nn.gelu(_in_proj(h, win_ref, 0))
    v = _rms(jax.nn.gelu(_in_proj(h, win_ref, 1)), vng_ref[...]).astype(BF16)
    causal = (lax.broadcasted_iota(jnp.int32, (CHUNK, CHUNK), 0)
              >= lax.broadcasted_iota(jnp.int32, (CHUNK, CHUNK), 1))
    for g in range(A_GROUPS):
        w_g = jnp.where(causal, ws_ref[g], 0.0).astype(BF16)
        bias_g = bs_ref[:, g:g + 1]
        cols = slice(g * A_GROUP_DIM, (g + 1) * A_GROUP_DIM)
        for c in range(tm // CHUNK):
            rows = slice(c * CHUNK, (c + 1) * CHUNK)
            s = jnp.dot(w_g, v[rows, cols], preferred_element_type=F32) + bias_g
            ya_scr[rows, cols] = (u[rows, cols] * s).astype(BF16)

    xb = _in_proj(h, win_ref, 2)
    xb_scr[HIST:HIST + tm, :] = xb
    xc = cb_ref[...] + cw_ref[CONV_WIDTH - 1:CONV_WIDTH, :] * xb
    for k in range(CONV_WIDTH - 1):
        back = CONV_WIDTH - 1 - k
        xc = xc + cw_ref[k:k + 1, :] * xb_scr[HIST - back:HIST - back + tm, :]
    convnew_ref[...] = xb_scr[HIST + tm - (CONV_WIDTH - 1):HIST + tm, :]
    xb_scr[0:HIST, :] = xb_scr[tm:tm + HIST, :]

    a, b = _lru_coeffs(xc, wr_ref, br_ref, wi_ref, bi_ref, lam_ref)
    a_scr[...] = a
    b_scr[...] = b

    row_id = lax.broadcasted_iota(jnp.int32, (SUBLANES, D_MODEL), 0)

    def group(i, carry):
        r0 = pl.multiple_of(i * SUBLANES, SUBLANES)
        ag = a_scr[pl.ds(r0, SUBLANES), :]
        bg = b_scr[pl.ds(r0, SUBLANES), :]
        d = 1
        while d < SUBLANES:
            keep = row_id >= d
            a_prev = jnp.where(keep, pltpu.roll(ag, d, 0), 1.0)
            b_prev = jnp.where(keep, pltpu.roll(bg, d, 0), 0.0)
            bg = ag * b_prev + bg
            ag = ag * a_prev
            d *= 2
        hg = ag * carry + bg
        b_scr[pl.ds(r0, SUBLANES), :] = hg
        return hg[SUBLANES - 1:SUBLANES, :]

    h_end = lax.fori_loop(0, tm // SUBLANES, group, h_scr[...], unroll=4)
    h_scr[...] = h_end
    hlast_ref[...] = h_end

    yb = b_scr[...] * jax.nn.gelu(_in_proj(h, win_ref, 3))
    xo_ref[...] = _merge(x, ya_scr[...], yb, _in_proj(h, win_ref, 4), _in_proj(h, win_ref, 5),
                         pa_ref, pb_ref, wo_ref)


def _mixer_prompt(x, weights):
    bsz, seq, d = x.shape
    tm = MIX_TILE
    weight_bytes = sum(w.size * w.dtype.itemsize for w in weights)
    row = pl.BlockSpec((None, tm, d), lambda b, t: (b, t, 0))
    per_seq = lambda rows: pl.BlockSpec((None, rows, d), lambda b, t: (b, 0, 0))
    return pl.pallas_call(
        _mixer_prompt_kernel,
        grid=(bsz, seq // tm),
        in_specs=[row] + [_resident(w.shape) for w in weights],
        out_specs=[row, per_seq(1), per_seq(CONV_WIDTH - 1)],
        out_shape=[jax.ShapeDtypeStruct((bsz, seq, d), F32),
                   jax.ShapeDtypeStruct((bsz, 1, d), F32),
                   jax.ShapeDtypeStruct((bsz, CONV_WIDTH - 1, d), F32)],
        scratch_shapes=[pltpu.VMEM((HIST + tm, d), F32),
                        pltpu.VMEM((tm, d), F32),
                        pltpu.VMEM((tm, d), F32),
                        pltpu.VMEM((tm, d), BF16),
                        pltpu.VMEM((1, d), F32)],
        compiler_params=pltpu.CompilerParams(
            dimension_semantics=("arbitrary", "arbitrary"),
            vmem_limit_bytes=_vmem_limit(weight_bytes, tm, 24 * d)),
    )(x, *weights)


def _mixer_sample_kernel(x_ref, h0_ref, cprev_ref, mixg_ref, win_ref, vng_ref, ws0_ref, bs0_ref,
                         cw_ref, cb_ref, wr_ref, br_ref, wi_ref, bi_ref, lam_ref,
                         pa_ref, pb_ref, wo_ref,
                         xo_ref, hnew_ref, xb_ref, v_ref):
    x = x_ref[...]
    h = _rms(x, mixg_ref[...]).astype(BF16)

    u = jax.nn.gelu(_in_proj(h, win_ref, 0))
    v = _rms(jax.nn.gelu(_in_proj(h, win_ref, 1)), vng_ref[...])
    v_ref[...] = v
    ya = u * (ws0_ref[...] * v + bs0_ref[...])

    xb = _in_proj(h, win_ref, 2)
    xb_ref[...] = xb
    xc = cb_ref[...] + cw_ref[CONV_WIDTH - 1:CONV_WIDTH, :] * xb
    for k in range(CONV_WIDTH - 1):
        xc = xc + cw_ref[k:k + 1, :] * cprev_ref[k]

    a, b = _lru_coeffs(xc, wr_ref, br_ref, wi_ref, bi_ref, lam_ref)
    h_new = a * h0_ref[...] + b
    hnew_ref[...] = h_new

    yb = h_new * jax.nn.gelu(_in_proj(h, win_ref, 3))
    xo_ref[...] = _merge(x, ya.astype(BF16), yb, _in_proj(h, win_ref, 4), _in_proj(h, win_ref, 5),
                         pa_ref, pb_ref, wo_ref)


def _mixer_sample(x2d, h0, cprev, weights):
    m, d = x2d.shape
    args = (x2d, h0, cprev) + tuple(weights)
    weight_bytes = sum(w.size * w.dtype.itemsize for w in weights)
    out = jax.ShapeDtypeStruct((m, d), F32)
    return pl.pallas_call(
        _mixer_sample_kernel,
        grid=(1,),
        in_specs=[_resident(a.shape) for a in args],
        out_specs=[pl.BlockSpec((m, d), lambda i: (0, 0))] * 4,
        out_shape=[out] * 4,
        compiler_params=pltpu.CompilerParams(
            dimension_semantics=("arbitrary",),
            vmem_limit_bytes=_vmem_limit(weight_bytes, m, 32 * d)),
    )(*args)


def _block_diag_tiles(w):
    w = w.reshape(GATE_BLOCKS, HEADS_PER_BLOCK, B_HEAD_DIM, B_HEAD_DIM)
    eye = jnp.eye(HEADS_PER_BLOCK, dtype=w.dtype)
    t = jnp.einsum('bhij,hk->bhikj', w, eye)
    return t.reshape(GATE_BLOCKS, MXU_DIM, MXU_DIM)


def kernel(x_prompt, x_sample, state_lru_h, state_conv, ffn1_norm, ffn1_w_gate, ffn1_w_up, ffn1_w_down, mix_norm, w_in, gmlp_v_norm, spatial_w, spatial_b, conv_w, conv_b, lru_w_r, lru_b_r, lru_w_i, lru_b_i, lru_lambda, proj_a, proj_b, w_out, ffn2_norm, ffn2_w_gate, ffn2_w_up, ffn2_w_down, final_norm):
    depth = ffn1_norm.shape[0]
    bsz, seq, d = x_prompt.shape
    nsamp = x_sample.shape[0]
    row = lambda p: p.reshape(1, -1)
    fin = row(final_norm)

    xp = x_prompt
    xs = x_sample.reshape(nsamp, d)
    h_p, c_p, h_s, c_s, v_s = [], [], [], [], []
    for l in range(depth):
        ffn1 = (row(ffn1_norm[l]), ffn1_w_gate[l].astype(BF16), ffn1_w_up[l].astype(BF16),
                ffn1_w_down[l].astype(BF16), fin)
        ffn2 = (row(ffn2_norm[l]), ffn2_w_gate[l].astype(BF16), ffn2_w_up[l].astype(BF16),
                ffn2_w_down[l].astype(BF16), fin)
        gate_common = (row(conv_b[l]),
                       _block_diag_tiles(lru_w_r[l]).astype(BF16), row(lru_b_r[l]),
                       _block_diag_tiles(lru_w_i[l]).astype(BF16), row(lru_b_i[l]),
                       row(lru_lambda[l]),
                       proj_a[l].astype(BF16), proj_b[l].astype(BF16), w_out[l].astype(BF16))
        head = (row(mix_norm[l]), w_in[l].astype(BF16), row(gmlp_v_norm[l]))
        w_prompt = head + (spatial_w[l], spatial_b[l].T, conv_w[l]) + gate_common
        w_sample = head + (row(jnp.repeat(spatial_w[l][:, 0, 0], A_GROUP_DIM)),
                           row(jnp.repeat(spatial_b[l][:, 0], A_GROUP_DIM)), conv_w[l]) + gate_common
        last = l == depth - 1

        xp = _ffn(xp.reshape(bsz * seq, d), *ffn1, final_norm=False).reshape(bsz, seq, d)
        xp, hl, cn = _mixer_prompt(xp, w_prompt)
        xp = _ffn(xp.reshape(bsz * seq, d), *ffn2, final_norm=last).reshape(bsz, seq, d)
        h_p.append(hl.reshape(bsz, d))
        c_p.append(cn)

        cprev = jnp.swapaxes(state_conv[l], 0, 1)
        xs = _ffn(xs, *ffn1, final_norm=False)
        xs, hn, xb_new, v_new = _mixer_sample(xs, state_lru_h[l], cprev, w_sample)
        xs = _ffn(xs, *ffn2, final_norm=last)
        h_s.append(hn)
        c_s.append(jnp.concatenate([state_conv[l][:, 1:], xb_new[:, None, :]], axis=1))
        v_s.append(v_new.reshape(nsamp, 1, d))

    return (xp, xs.reshape(nsamp, 1, d), jnp.stack(h_p), jnp.stack(c_p),
            jnp.stack(h_s), jnp.stack(c_s), jnp.stack(v_s))
```

```python
import functools

import jax
import jax.numpy as jnp
from jax import lax
from jax.experimental import pallas as pl
from jax.experimental.pallas import tpu as pltpu

D_MODEL = 1024
CHUNK = 128
A_GROUPS = 8
A_GROUP_DIM = D_MODEL // A_GROUPS
B_HEADS = 16
B_HEAD_DIM = D_MODEL // B_HEADS
CONV_WIDTH = 4
LRU_C = 8.0
EPS = 1e-6

SUBLANES = 8
MXU_DIM = 256
GATE_BLOCKS = D_MODEL // MXU_DIM
HEADS_PER_BLOCK = MXU_DIM // B_HEAD_DIM

FFN_TILE = 512
MIX_TILE = 512
HIST = SUBLANES

F32 = jnp.float32
BF16 = jnp.bfloat16


def _vmem_limit(weight_bytes, tile_rows, live_f32_cols):
    act = tile_rows * live_f32_cols * 4
    return int(1.25 * (weight_bytes + act)) + (4 << 20)


def _rms(x, g):
    return x * lax.rsqrt(jnp.mean(x * x, axis=-1, keepdims=True) + EPS) * g


def _resident(shape):
    nd = len(shape)
    return pl.BlockSpec(shape, lambda *_: (0,) * nd, pipeline_mode=pl.Buffered(1))


def _ffn_kernel(x_ref, g_ref, wg_ref, wu_ref, wd_ref, fg_ref, o_ref, *, final_norm):
    x = x_ref[...]
    h = _rms(x, g_ref[...]).astype(BF16)
    gate = jnp.dot(h, wg_ref[...], preferred_element_type=F32)
    up = jnp.dot(h, wu_ref[...], preferred_element_type=F32)
    act = (jax.nn.silu(gate) * up).astype(BF16)
    y = x + 0.5 * jnp.dot(act, wd_ref[...], preferred_element_type=F32)
    if final_norm:
        y = _rms(y, fg_ref[...])
    o_ref[...] = y


def _ffn(x2d, g, wg, wu, wd, fg, *, final_norm):
    m, d = x2d.shape
    dff = wg.shape[1]
    tm = min(FFN_TILE, m)
    row = pl.BlockSpec((tm, d), lambda i: (i, 0))
    weight_bytes = 2 * (wg.size + wu.size + wd.size)
    return pl.pallas_call(
        functools.partial(_ffn_kernel, final_norm=final_norm),
        grid=(m // tm,),
        in_specs=[row, _resident(g.shape), _resident(wg.shape), _resident(wu.shape),
                  _resident(wd.shape), _resident(fg.shape)],
        out_specs=row,
        out_shape=jax.ShapeDtypeStruct((m, d), F32),
        compiler_params=pltpu.CompilerParams(
            dimension_semantics=("arbitrary",),
            vmem_limit_bytes=_vmem_limit(weight_bytes, tm, 4 * d + 4 * dff)),
    )(x2d, g, wg, wu, wd, fg)


def _in_proj(h, win_ref, k):
    return jnp.dot(h, win_ref[:, k * D_MODEL:(k + 1) * D_MODEL], preferred_element_type=F32)


def _lru_coeffs(xc, wr_ref, br_ref, wi_ref, bi_ref, lam_ref):
    xcb = xc.astype(BF16)
    r_parts, i_parts = [], []
    for j in range(GATE_BLOCKS):
        blk = xcb[:, j * MXU_DIM:(j + 1) * MXU_DIM]
        r_parts.append(jnp.dot(blk, wr_ref[j], preferred_element_type=F32))
        i_parts.append(jnp.dot(blk, wi_ref[j], preferred_element_type=F32))
    r = jax.nn.sigmoid(jnp.concatenate(r_parts, axis=-1) + br_ref[...])
    i = jax.nn.sigmoid(jnp.concatenate(i_parts, axis=-1) + bi_ref[...])
    lam = lam_ref[...]
    softplus_neg_lam = jnp.maximum(-lam, 0.0) + jnp.log1p(jnp.exp(-jnp.abs(lam)))
    log_a = -LRU_C * r * softplus_neg_lam
    a = jnp.exp(log_a)
    th = jnp.tanh(log_a)
    b = jnp.sqrt(-2.0 * th / (1.0 - th)) * (i * xc)
    return a, b


def _merge(x, ya_b16, yb, g_a, g_b, pa_ref, pb_ref, wo_ref):
    m = (jax.nn.sigmoid(g_a) * jnp.dot(ya_b16, pa_ref[...], preferred_element_type=F32)
         + jax.nn.sigmoid(g_b) * jnp.dot(yb.astype(BF16), pb_ref[...], preferred_element_type=F32))
    return x + jnp.dot(m.astype(BF16), wo_ref[...], preferred_element_type=F32)


def _mixer_prompt_kernel(x_ref, mixg_ref, win_ref, vng_ref, ws_ref, bs_ref, cw_ref, cb_ref,
                         wr_ref, br_ref, wi_ref, bi_ref, lam_ref, pa_ref, pb_ref, wo_ref,
                         xo_ref, hlast_ref, convnew_ref,
                         xb_scr, a_scr, b_scr, ya_scr, h_scr):
    tm = x_ref.shape[0]

    @pl.when(pl.program_id(1) == 0)
    def _():
        xb_scr[0:HIST, :] = jnp.zeros((HIST, D_MODEL), F32)
        h_scr[...] = jnp.zeros((1, D_MODEL), F32)

    x = x_ref[...]
    h = _rms(x, mixg_ref[...]).astype(BF16)

    u = jax.nn.gelu(_in_proj(h, win_ref, 0))
    v = _rms(jax.nn.gelu(_in_proj(h, win_ref, 1)), vng_ref[...]).astype(BF16)
    causal = (lax.broadcasted_iota(jnp.int32, (CHUNK, CHUNK), 0)
              >= lax.broadcasted_iota(jnp.int32, (CHUNK, CHUNK), 1))
    for g in range(A_GROUPS):
        w_g = jnp.where(causal, ws_ref[g], 0.0).astype(BF16)
        bias_g = bs_ref[:, g:g + 1]
        cols = slice(g * A_GROUP_DIM, (g + 1) * A_GROUP_DIM)
        for c in range(tm // CHUNK):
            rows = slice(c * CHUNK, (c + 1) * CHUNK)
            s = jnp.dot(w_g, v[rows, cols], preferred_element_type=F32) + bias_g
            ya_scr[rows, cols] = (u[rows, cols] * s).astype(BF16)

    xb = _in_proj(h, win_ref, 2)
    xb_scr[HIST:HIST + tm, :] = xb
    xc = cb_ref[...] + cw_ref[CONV_WIDTH - 1:CONV_WIDTH, :] * xb
    for k in range(CONV_WIDTH - 1):
        back = CONV_WIDTH - 1 - k
        xc = xc + cw_ref[k:k + 1, :] * xb_scr[HIST - back:HIST - back + tm, :]
    convnew_ref[...] = xb_scr[HIST + tm - (CONV_WIDTH - 1):HIST + tm, :]
    xb_scr[0:HIST, :] = xb_scr[tm:tm + HIST, :]

    a, b = _lru_coeffs(xc, wr_ref, br_ref, wi_ref, bi_ref, lam_ref)
    a_scr[...] = a
    b_scr[...] = b

    first_row = lax.broadcasted_iota(jnp.int32, (SUBLANES, D_MODEL), 0) == 0
    carry = h_scr[...]
    for r0 in range(0, tm, SUBLANES):
        ag = a_scr[r0:r0 + SUBLANES, :]
        hg = jnp.where(first_row, ag, 0.0) * carry + b_scr[r0:r0 + SUBLANES, :]
        span = jnp.where(first_row, 0.0, ag)
        d = 1
        while d < SUBLANES:
            hg = span * pltpu.roll(hg, d, 0) + hg
            if 2 * d < SUBLANES:
                span = span * pltpu.roll(span, d, 0)
            d *= 2
        b_scr[r0:r0 + SUBLANES, :] = hg
        carry = hg[SUBLANES - 1:SUBLANES, :]
    h_scr[...] = carry
    hlast_ref[...] = carry

    yb = b_scr[...] * jax.nn.gelu(_in_proj(h, win_ref, 3))
    xo_ref[...] = _merge(x, ya_scr[...], yb, _in_proj(h, win_ref, 4), _in_proj(h, win_ref, 5),
                         pa_ref, pb_ref, wo_ref)


def _mixer_prompt(x, weights):
    bsz, seq, d = x.shape
    tm = MIX_TILE
    weight_bytes = sum(w.size * w.dtype.itemsize for w in weights)
    row = pl.BlockSpec((None, tm, d), lambda b, t: (b, t, 0))
    per_seq = lambda rows: pl.BlockSpec((None, rows, d), lambda b, t: (b, 0, 0))
    return pl.pallas_call(
        _mixer_prompt_kernel,
        grid=(bsz, seq // tm),
        in_specs=[row] + [_resident(w.shape) for w in weights],
        out_specs=[row, per_seq(1), per_seq(CONV_WIDTH - 1)],
        out_shape=[jax.ShapeDtypeStruct((bsz, seq, d), F32),
                   jax.ShapeDtypeStruct((bsz, 1, d), F32),
                   jax.ShapeDtypeStruct((bsz, CONV_WIDTH - 1, d), F32)],
        scratch_shapes=[pltpu.VMEM((HIST + tm, d), F32),
                        pltpu.VMEM((tm, d), F32),
                        pltpu.VMEM((tm, d), F32),
                        pltpu.VMEM((tm, d), BF16),
                        pltpu.VMEM((1, d), F32)],
        compiler_params=pltpu.CompilerParams(
            dimension_semantics=("arbitrary", "arbitrary"),
            vmem_limit_bytes=_vmem_limit(weight_bytes, tm, 24 * d)),
    )(x, *weights)


def _mixer_sample_kernel(x_ref, h0_ref, cprev_ref, mixg_ref, win_ref, vng_ref, ws0_ref, bs0_ref,
                         cw_ref, cb_ref, wr_ref, br_ref, wi_ref, bi_ref, lam_ref,
                         pa_ref, pb_ref, wo_ref,
                         xo_ref, hnew_ref, xb_ref, v_ref):
    x = x_ref[...]
    h = _rms(x, mixg_ref[...]).astype(BF16)

    u = jax.nn.gelu(_in_proj(h, win_ref, 0))
    v = _rms(jax.nn.gelu(_in_proj(h, win_ref, 1)), vng_ref[...])
    v_ref[...] = v
    ya = u * (ws0_ref[...] * v + bs0_ref[...])

    xb = _in_proj(h, win_ref, 2)
    xb_ref[...] = xb
    xc = cb_ref[...] + cw_ref[CONV_WIDTH - 1:CONV_WIDTH, :] * xb
    for k in range(CONV_WIDTH - 1):
        xc = xc + cw_ref[k:k + 1, :] * cprev_ref[k]

    a, b = _lru_coeffs(xc, wr_ref, br_ref, wi_ref, bi_ref, lam_ref)
    h_new = a * h0_ref[...] + b
    hnew_ref[...] = h_new

    yb = h_new * jax.nn.gelu(_in_proj(h, win_ref, 3))
    xo_ref[...] = _merge(x, ya.astype(BF16), yb, _in_proj(h, win_ref, 4), _in_proj(h, win_ref, 5),
                         pa_ref, pb_ref, wo_ref)


def _mixer_sample(x2d, h0, cprev, weights):
    m, d = x2d.shape
    args = (x2d, h0, cprev) + tuple(weights)
    weight_bytes = sum(w.size * w.dtype.itemsize for w in weights)
    out = jax.ShapeDtypeStruct((m, d), F32)
    return pl.pallas_call(
        _mixer_sample_kernel,
        grid=(1,),
        in_specs=[_resident(a.shape) for a in args],
        out_specs=[pl.BlockSpec((m, d), lambda i: (0, 0))] * 4,
        out_shape=[out] * 4,
        compiler_params=pltpu.CompilerParams(
            dimension_semantics=("arbitrary",),
            vmem_limit_bytes=_vmem_limit(weight_bytes, m, 32 * d)),
    )(*args)


def _block_diag_tiles(w):
    w = w.reshape(GATE_BLOCKS, HEADS_PER_BLOCK, B_HEAD_DIM, B_HEAD_DIM)
    eye = jnp.eye(HEADS_PER_BLOCK, dtype=w.dtype)
    t = jnp.einsum('bhij,hk->bhikj', w, eye)
    return t.reshape(GATE_BLOCKS, MXU_DIM, MXU_DIM)


def kernel(x_prompt, x_sample, state_lru_h, state_conv, ffn1_norm, ffn1_w_gate, ffn1_w_up, ffn1_w_down, mix_norm, w_in, gmlp_v_norm, spatial_w, spatial_b, conv_w, conv_b, lru_w_r, lru_b_r, lru_w_i, lru_b_i, lru_lambda, proj_a, proj_b, w_out, ffn2_norm, ffn2_w_gate, ffn2_w_up, ffn2_w_down, final_norm):
    depth = ffn1_norm.shape[0]
    bsz, seq, d = x_prompt.shape
    nsamp = x_sample.shape[0]
    row = lambda p: p.reshape(1, -1)
    fin = row(final_norm)

    xp = x_prompt
    xs = x_sample.reshape(nsamp, d)
    h_p, c_p, h_s, c_s, v_s = [], [], [], [], []
    for l in range(depth):
        ffn1 = (row(ffn1_norm[l]), ffn1_w_gate[l].astype(BF16), ffn1_w_up[l].astype(BF16),
                ffn1_w_down[l].astype(BF16), fin)
        ffn2 = (row(ffn2_norm[l]), ffn2_w_gate[l].astype(BF16), ffn2_w_up[l].astype(BF16),
                ffn2_w_down[l].astype(BF16), fin)
        gate_common = (row(conv_b[l]),
                       _block_diag_tiles(lru_w_r[l]).astype(BF16), row(lru_b_r[l]),
                       _block_diag_tiles(lru_w_i[l]).astype(BF16), row(lru_b_i[l]),
                       row(lru_lambda[l]),
                       proj_a[l].astype(BF16), proj_b[l].astype(BF16), w_out[l].astype(BF16))
        head = (row(mix_norm[l]), w_in[l].astype(BF16), row(gmlp_v_norm[l]))
        w_prompt = head + (spatial_w[l], spatial_b[l].T, conv_w[l]) + gate_common
        w_sample = head + (row(jnp.repeat(spatial_w[l][:, 0, 0], A_GROUP_DIM)),
                           row(jnp.repeat(spatial_b[l][:, 0], A_GROUP_DIM)), conv_w[l]) + gate_common
        last = l == depth - 1

        xp = _ffn(xp.reshape(bsz * seq, d), *ffn1, final_norm=False).reshape(bsz, seq, d)
        xp, hl, cn = _mixer_prompt(xp, w_prompt)
        xp = _ffn(xp.reshape(bsz * seq, d), *ffn2, final_norm=last).reshape(bsz, seq, d)
        h_p.append(hl.reshape(bsz, d))
        c_p.append(cn)

        cprev = jnp.swapaxes(state_conv[l], 0, 1)
        xs = _ffn(xs, *ffn1, final_norm=False)
        xs, hn, xb_new, v_new = _mixer_sample(xs, state_lru_h[l], cprev, w_sample)
        xs = _ffn(xs, *ffn2, final_norm=last)
        h_s.append(hn)
        c_s.append(jnp.concatenate([state_conv[l][:, 1:], xb_new[:, None, :]], axis=1))
        v_s.append(v_new.reshape(nsamp, 1, d))

    return (xp, xs.reshape(nsamp, 1, d), jnp.stack(h_p), jnp.stack(c_p),
            jnp.stack(h_s), jnp.stack(c_s), jnp.stack(v_s))
```

```python
import functools

import jax
import jax.numpy as jnp
from jax import lax
from jax.experimental import pallas as pl
from jax.experimental.pallas import tpu as pltpu

D_MODEL = 1024
CHUNK = 128
A_GROUPS = 8
A_GROUP_DIM = D_MODEL // A_GROUPS
B_HEADS = 16
B_HEAD_DIM = D_MODEL // B_HEADS
CONV_WIDTH = 4
LRU_C = 8.0
EPS = 1e-6
IN_SEGMENTS = 6

SUBLANES = 8
BF16_ROWS = 16
MXU_DIM = 256
GATE_BLOCKS = D_MODEL // MXU_DIM
HEADS_PER_BLOCK = MXU_DIM // B_HEAD_DIM

FFN_TILE = 512
MIX_TILE = 512
HIST = SUBLANES
STAGE_SLOTS = 2
STAGE_BYTES = 3 << 20

F32 = jnp.float32
BF16 = jnp.bfloat16


def _rms(x, g):
    return x * lax.rsqrt(jnp.mean(x * x, axis=-1, keepdims=True) + EPS) * g


def _resident(shape):
    nd = len(shape)
    return pl.BlockSpec(shape, lambda *_: (0,) * nd, pipeline_mode=pl.Buffered(1))


_HBM = pl.BlockSpec(memory_space=pl.ANY)


def _chunk_rows(k, n):
    best = BF16_ROWS
    for r in range(BF16_ROWS, k + 1, BF16_ROWS):
        if k % r == 0 and r * n * 4 <= STAGE_BYTES:
            best = r
    return best


def _stage_shape(k, n):
    return (STAGE_SLOTS, _chunk_rows(k, n), n)


def _stream_weights_to_bf16(jobs, sems):
    chunks = []
    for src, dst, stage in jobs:
        rows = stage.shape[1]
        for c in range(src.shape[0] // rows):
            chunks.append((src, dst, stage, c * rows, rows))

    def copy(j):
        src, _, stage, r0, rows = chunks[j]
        slot = j % STAGE_SLOTS
        return pltpu.make_async_copy(src.at[pl.ds(r0, rows), :], stage.at[slot], sems.at[slot])

    copy(0).start()
    for j, (_, dst, stage, r0, rows) in enumerate(chunks):
        if j + 1 < len(chunks):
            copy(j + 1).start()
        copy(j).wait()
        dst[r0:r0 + rows, :] = stage[j % STAGE_SLOTS].astype(BF16)


def _ffn_rows(x, g_ref, wg_v, wu_v, wd_v, fg_ref, final_norm):
    h = _rms(x, g_ref[...]).astype(BF16)
    gate = jnp.dot(h, wg_v[...], preferred_element_type=F32)
    up = jnp.dot(h, wu_v[...], preferred_element_type=F32)
    act = (jax.nn.silu(gate) * up).astype(BF16)
    y = x + 0.5 * jnp.dot(act, wd_v[...], preferred_element_type=F32)
    return _rms(y, fg_ref[...]) if final_norm else y


def _ffn_kernel(xp_ref, xs_ref, g_ref, fg_ref, wg_hbm, wu_hbm, wd_hbm,
                op_ref, os_ref,
                wg_v, wu_v, wd_v, stage_up, stage_dn, sems, *, n_tiles, final_norm):
    step = pl.program_id(0)

    @pl.when(step == 0)
    def _():
        _stream_weights_to_bf16([(wg_hbm, wg_v, stage_up), (wu_hbm, wu_v, stage_up),
                                 (wd_hbm, wd_v, stage_dn)], sems)

    @pl.when(step < n_tiles)
    def _():
        op_ref[...] = _ffn_rows(xp_ref[...], g_ref, wg_v, wu_v, wd_v, fg_ref, final_norm)

    @pl.when(step == n_tiles)
    def _():
        os_ref[...] = _ffn_rows(xs_ref[...], g_ref, wg_v, wu_v, wd_v, fg_ref, final_norm)


def _ffn(xp2d, xs2d, g, fg, wg, wu, wd, *, final_norm):
    m, d = xp2d.shape
    ms = xs2d.shape[0]
    dff = wg.shape[1]
    tm = FFN_TILE
    n_tiles = m // tm
    prompt_rows = pl.BlockSpec((tm, d), lambda i: (jnp.minimum(i, n_tiles - 1), 0))
    sample_rows = pl.BlockSpec((ms, d), lambda i: (0, 0))
    stage_up, stage_dn = _stage_shape(d, dff), _stage_shape(dff, d)
    vmem = (2 * 3 * d * dff
            + 4 * (stage_up[0] * stage_up[1] * stage_up[2] + stage_dn[0] * stage_dn[1] * stage_dn[2])
            + 4 * 4 * (tm + ms) * d
            + tm * (2 * 4 + 2 + 2) * dff + 4 * 4 * tm * d)
    return pl.pallas_call(
        functools.partial(_ffn_kernel, n_tiles=n_tiles, final_norm=final_norm),
        grid=(n_tiles + 1,),
        in_specs=[prompt_rows, sample_rows, _resident(g.shape), _resident(fg.shape), _HBM, _HBM, _HBM],
        out_specs=[prompt_rows, sample_rows],
        out_shape=[jax.ShapeDtypeStruct((m, d), F32), jax.ShapeDtypeStruct((ms, d), F32)],
        scratch_shapes=[pltpu.VMEM((d, dff), BF16), pltpu.VMEM((d, dff), BF16), pltpu.VMEM((dff, d), BF16),
                        pltpu.VMEM(stage_up, F32), pltpu.VMEM(stage_dn, F32),
                        pltpu.SemaphoreType.DMA((STAGE_SLOTS,))],
        compiler_params=pltpu.CompilerParams(dimension_semantics=("arbitrary",), vmem_limit_bytes=vmem),
    )(xp2d, xs2d, g, fg, wg, wu, wd)


def _in_proj(h, win_v, k):
    return jnp.dot(h, win_v[:, k * D_MODEL:(k + 1) * D_MODEL], preferred_element_type=F32)


def _lru_coeffs(xc, wr_ref, br_ref, wi_ref, bi_ref, lam_ref):
    xcb = xc.astype(BF16)
    r_parts, i_parts = [], []
    for j in range(GATE_BLOCKS):
        blk = xcb[:, j * MXU_DIM:(j + 1) * MXU_DIM]
        r_parts.append(jnp.dot(blk, wr_ref[j], preferred_element_type=F32))
        i_parts.append(jnp.dot(blk, wi_ref[j], preferred_element_type=F32))
    r = jax.nn.sigmoid(jnp.concatenate(r_parts, axis=-1) + br_ref[...])
    i = jax.nn.sigmoid(jnp.concatenate(i_parts, axis=-1) + bi_ref[...])
    lam = lam_ref[...]
    softplus_neg_lam = jnp.maximum(-lam, 0.0) + jnp.log1p(jnp.exp(-jnp.abs(lam)))
    log_a = -LRU_C * r * softplus_neg_lam
    a = jnp.exp(log_a)
    th = jnp.tanh(log_a)
    b = jnp.sqrt(-2.0 * th / (1.0 - th)) * (i * xc)
    return a, b


def _merge(x, ya_b16, yb, g_a, g_b, pa_v, pb_v, wo_v):
    m = (jax.nn.sigmoid(g_a) * jnp.dot(ya_b16, pa_v[...], preferred_element_type=F32)
         + jax.nn.sigmoid(g_b) * jnp.dot(yb.astype(BF16), pb_v[...], preferred_element_type=F32))
    return x + jnp.dot(m.astype(BF16), wo_v[...], preferred_element_type=F32)


def _mixer_prompt_tile(x_ref, p, win_v, pa_v, pb_v, wo_v, xo_ref, hlast_ref, convnew_ref,
                       xb_scr, a_scr, b_scr, ya_scr, h_scr, first_tile):
    tm = x_ref.shape[0]

    @pl.when(first_tile)
    def _():
        xb_scr[0:HIST, :] = jnp.zeros((HIST, D_MODEL), F32)
        h_scr[...] = jnp.zeros((1, D_MODEL), F32)

    x = x_ref[...]
    h = _rms(x, p["mix_norm"][...]).astype(BF16)

    u = jax.nn.gelu(_in_proj(h, win_v, 0))
    v = _rms(jax.nn.gelu(_in_proj(h, win_v, 1)), p["v_norm"][...]).astype(BF16)
    causal = (lax.broadcasted_iota(jnp.int32, (CHUNK, CHUNK), 0)
              >= lax.broadcasted_iota(jnp.int32, (CHUNK, CHUNK), 1))
    for g in range(A_GROUPS):
        w_g = jnp.where(causal, p["spatial_w"][g], 0.0).astype(BF16)
        bias_g = p["spatial_b_t"][:, g:g + 1]
        cols = slice(g * A_GROUP_DIM, (g + 1) * A_GROUP_DIM)
        for c in range(tm // CHUNK):
            rows = slice(c * CHUNK, (c + 1) * CHUNK)
            s = jnp.dot(w_g, v[rows, cols], preferred_element_type=F32) + bias_g
            ya_scr[rows, cols] = (u[rows, cols] * s).astype(BF16)

    xb = _in_proj(h, win_v, 2)
    xb_scr[HIST:HIST + tm, :] = xb
    cw = p["conv_w"]
    xc = p["conv_b"][...] + cw[CONV_WIDTH - 1:CONV_WIDTH, :] * xb
    for k in range(CONV_WIDTH - 1):
        back = CONV_WIDTH - 1 - k
        xc = xc + cw[k:k + 1, :] * xb_scr[HIST - back:HIST - back + tm, :]
    convnew_ref[...] = xb_scr[HIST + tm - (CONV_WIDTH - 1):HIST + tm, :]
    xb_scr[0:HIST, :] = xb_scr[tm:tm + HIST, :]

    a, b = _lru_coeffs(xc, p["w_r"], p["b_r"], p["w_i"], p["b_i"], p["lam"])
    a_scr[...] = a
    b_scr[...] = b

    first_row = lax.broadcasted_iota(jnp.int32, (SUBLANES, D_MODEL), 0) == 0
    carry = h_scr[...]
    for r0 in range(0, tm, SUBLANES):
        ag = a_scr[r0:r0 + SUBLANES, :]
        hg = jnp.where(first_row, ag, 0.0) * carry + b_scr[r0:r0 + SUBLANES, :]
        span = jnp.where(first_row, 0.0, ag)
        d = 1
        while d < SUBLANES:
            hg = span * pltpu.roll(hg, d, 0) + hg
            if 2 * d < SUBLANES:
                span = span * pltpu.roll(span, d, 0)
            d *= 2
        b_scr[r0:r0 + SUBLANES, :] = hg
        carry = hg[SUBLANES - 1:SUBLANES, :]
    h_scr[...] = carry
    hlast_ref[...] = carry

    yb = b_scr[...] * jax.nn.gelu(_in_proj(h, win_v, 3))
    xo_ref[...] = _merge(x, ya_scr[...], yb, _in_proj(h, win_v, 4), _in_proj(h, win_v, 5),
                         pa_v, pb_v, wo_v)


def _mixer_sample_rows(x_ref, h0_ref, cprev_ref, p, win_v, pa_v, pb_v, wo_v,
                       xo_ref, hnew_ref, xbnew_ref, v_ref):
    x = x_ref[...]
    h = _rms(x, p["mix_norm"][...]).astype(BF16)

    u = jax.nn.gelu(_in_proj(h, win_v, 0))
    v = _rms(jax.nn.gelu(_in_proj(h, win_v, 1)), p["v_norm"][...])
    v_ref[...] = v
    ya = u * (p["spatial_w0"][...] * v + p["spatial_b0"][...])

    xb = _in_proj(h, win_v, 2)
    xbnew_ref[...] = xb
    cw = p["conv_w"]
    xc = p["conv_b"][...] + cw[CONV_WIDTH - 1:CONV_WIDTH, :] * xb
    for k in range(CONV_WIDTH - 1):
        xc = xc + cw[k:k + 1, :] * cprev_ref[k]

    a, b = _lru_coeffs(xc, p["w_r"], p["b_r"], p["w_i"], p["b_i"], p["lam"])
    h_new = a * h0_ref[...] + b
    hnew_ref[...] = h_new

    yb = h_new * jax.nn.gelu(_in_proj(h, win_v, 3))
    xo_ref[...] = _merge(x, ya.astype(BF16), yb, _in_proj(h, win_v, 4), _in_proj(h, win_v, 5),
                         pa_v, pb_v, wo_v)


_SMALL_PARAMS = ("mix_norm", "v_norm", "spatial_w", "spatial_b_t", "spatial_w0", "spatial_b0",
                 "conv_w", "conv_b", "w_r", "b_r", "w_i", "b_i", "lam")


def _mixer_kernel(*refs, n_tiles, tiles_per_seq):
    n_small = len(_SMALL_PARAMS)
    xp_ref, xs_ref, h0_ref, cprev_ref = refs[:4]
    p = dict(zip(_SMALL_PARAMS, refs[4:4 + n_small]))
    win_hbm, pa_hbm, pb_hbm, wo_hbm = refs[4 + n_small:8 + n_small]
    xo_p, hlast_ref, convnew_ref, xo_s, hnew_ref, xbnew_ref, v_ref = refs[8 + n_small:15 + n_small]
    (win_v, pa_v, pb_v, wo_v, stage_in, stage_sq, sems,
     xb_scr, a_scr, b_scr, ya_scr, h_scr) = refs[15 + n_small:]
    step = pl.program_id(0)

    @pl.when(step == 0)
    def _():
        _stream_weights_to_bf16([(win_hbm, win_v, stage_in), (pa_hbm, pa_v, stage_sq),
                                 (pb_hbm, pb_v, stage_sq), (wo_hbm, wo_v, stage_sq)], sems)

    @pl.when(step < n_tiles)
    def _():
        _mixer_prompt_tile(xp_ref, p, win_v, pa_v, pb_v, wo_v, xo_p, hlast_ref, convnew_ref,
                           xb_scr, a_scr, b_scr, ya_scr, h_scr, step % tiles_per_seq == 0)

    @pl.when(step == n_tiles)
    def _():
        _mixer_sample_rows(xs_ref, h0_ref, cprev_ref, p, win_v, pa_v, pb_v, wo_v,
                           xo_s, hnew_ref, xbnew_ref, v_ref)


def _mixer(xp, xs, h0, cprev, small, win, pa, pb, wo):
    bsz, seq, d = xp.shape
    ms = xs.shape[0]
    tm = MIX_TILE
    tiles_per_seq = seq // tm
    n_tiles = bsz * tiles_per_seq
    win_cols = win.shape[1]

    def tile_of(i):
        j = jnp.minimum(i, n_tiles - 1)
        return j // tiles_per_seq, j % tiles_per_seq

    prompt_rows = pl.BlockSpec((None, tm, d), lambda i: (*tile_of(i), 0))
    per_seq = lambda rows: pl.BlockSpec((None, rows, d), lambda i: (tile_of(i)[0], 0, 0))
    sample_rows = pl.BlockSpec((ms, d), lambda i: (0, 0))
    small_args = [small[k] for k in _SMALL_PARAMS]
    stage_in, stage_sq = _stage_shape(d, win_cols), _stage_shape(d, d)
    vmem = (2 * d * (win_cols + 3 * d)
            + 4 * (stage_in[0] * stage_in[1] * stage_in[2] + stage_sq[0] * stage_sq[1] * stage_sq[2])
            + sum(a.size * a.dtype.itemsize for a in small_args) * 2
            + 4 * 4 * tm * d + 4 * 16 * ms * d
            + (4 * 3 + 2) * tm * d + 4 * HIST * d
            + 4 * 4 * tm * d)
    return pl.pallas_call(
        functools.partial(_mixer_kernel, n_tiles=n_tiles, tiles_per_seq=tiles_per_seq),
        grid=(n_tiles + 1,),
        in_specs=[prompt_rows, sample_rows, sample_rows, _resident(cprev.shape)]
                 + [_resident(a.shape) for a in small_args] + [_HBM] * 4,
        out_specs=[prompt_rows, per_seq(1), per_seq(CONV_WIDTH - 1)] + [sample_rows] * 4,
        out_shape=[jax.ShapeDtypeStruct((bsz, seq, d), F32),
                   jax.ShapeDtypeStruct((bsz, 1, d), F32),
                   jax.ShapeDtypeStruct((bsz, CONV_WIDTH - 1, d), F32)]
                  + [jax.ShapeDtypeStruct((ms, d), F32)] * 4,
        scratch_shapes=[pltpu.VMEM((d, win_cols), BF16), pltpu.VMEM((d, d), BF16),
                        pltpu.VMEM((d, d), BF16), pltpu.VMEM((d, d), BF16),
                        pltpu.VMEM(stage_in, F32), pltpu.VMEM(stage_sq, F32),
                        pltpu.SemaphoreType.DMA((STAGE_SLOTS,)),
                        pltpu.VMEM((HIST + tm, d), F32),
                        pltpu.VMEM((tm, d), F32),
                        pltpu.VMEM((tm, d), F32),
                        pltpu.VMEM((tm, d), BF16),
                        pltpu.VMEM((1, d), F32)],
        compiler_params=pltpu.CompilerParams(dimension_semantics=("arbitrary",), vmem_limit_bytes=vmem),
    )(xp, xs, h0, cprev, *small_args, win, pa, pb, wo)


def _block_diag_tiles(w):
    w = w.reshape(GATE_BLOCKS, HEADS_PER_BLOCK, B_HEAD_DIM, B_HEAD_DIM)
    eye = jnp.eye(HEADS_PER_BLOCK, dtype=w.dtype)
    t = jnp.einsum('bhij,hk->bhikj', w, eye)
    return t.reshape(GATE_BLOCKS, MXU_DIM, MXU_DIM)


def kernel(x_prompt, x_sample, state_lru_h, state_conv, ffn1_norm, ffn1_w_gate, ffn1_w_up, ffn1_w_down, mix_norm, w_in, gmlp_v_norm, spatial_w, spatial_b, conv_w, conv_b, lru_w_r, lru_b_r, lru_w_i, lru_b_i, lru_lambda, proj_a, proj_b, w_out, ffn2_norm, ffn2_w_gate, ffn2_w_up, ffn2_w_down, final_norm):
    depth = ffn1_norm.shape[0]
    bsz, seq, d = x_prompt.shape
    nsamp = x_sample.shape[0]
    row = lambda p: p.reshape(1, -1)
    fin = row(final_norm)

    xp = x_prompt
    xs = x_sample.reshape(nsamp, d)
    h_p, c_p, h_s, c_s, v_s = [], [], [], [], []
    for l in range(depth):
        small = {
            "mix_norm": row(mix_norm[l]), "v_norm": row(gmlp_v_norm[l]),
            "spatial_w": spatial_w[l], "spatial_b_t": spatial_b[l].T,
            "spatial_w0": row(jnp.repeat(spatial_w[l][:, 0, 0], A_GROUP_DIM)),
            "spatial_b0": row(jnp.repeat(spatial_b[l][:, 0], A_GROUP_DIM)),
            "conv_w": conv_w[l], "conv_b": row(conv_b[l]),
            "w_r": _block_diag_tiles(lru_w_r[l]).astype(BF16), "b_r": row(lru_b_r[l]),
            "w_i": _block_diag_tiles(lru_w_i[l]).astype(BF16), "b_i": row(lru_b_i[l]),
            "lam": row(lru_lambda[l]),
        }
        last = l == depth - 1

        xp2d, xs = _ffn(xp.reshape(bsz * seq, d), xs, row(ffn1_norm[l]), fin,
                        ffn1_w_gate[l], ffn1_w_up[l], ffn1_w_down[l], final_norm=False)
        cprev = jnp.swapaxes(state_conv[l], 0, 1)
        xp, hl, cn, xs, hn, xb_new, v_new = _mixer(
            xp2d.reshape(bsz, seq, d), xs, state_lru_h[l], cprev, small,
            w_in[l], proj_a[l], proj_b[l], w_out[l])
        xp2d, xs = _ffn(xp.reshape(bsz * seq, d), xs, row(ffn2_norm[l]), fin,
                        ffn2_w_gate[l], ffn2_w_up[l], ffn2_w_down[l], final_norm=last)
        xp = xp2d.reshape(bsz, seq, d)

        h_p.append(hl.reshape(bsz, d))
        c_p.append(cn)
        h_s.append(hn)
        c_s.append(jnp.concatenate([state_conv[l][:, 1:], xb_new[:, None, :]], axis=1))
        v_s.append(v_new.reshape(nsamp, 1, d))

    return (xp, xs.reshape(nsamp, 1, d), jnp.stack(h_p), jnp.stack(c_p),
            jnp.stack(h_s), jnp.stack(c_s), jnp.stack(v_s))
```

```python
import functools

import jax
import jax.numpy as jnp
from jax import lax
from jax.experimental import pallas as pl
from jax.experimental.pallas import tpu as pltpu

D_MODEL = 1024
CHUNK = 128
A_GROUPS = 8
A_GROUP_DIM = D_MODEL // A_GROUPS
B_HEADS = 16
B_HEAD_DIM = D_MODEL // B_HEADS
CONV_WIDTH = 4
LRU_C = 8.0
EPS = 1e-6
IN_SEGMENTS = 6

SUBLANES = 8
BF16_ROWS = 16
MXU_DIM = 256
GATE_BLOCKS = D_MODEL // MXU_DIM
HEADS_PER_BLOCK = MXU_DIM // B_HEAD_DIM

FFN_TILE = 512
MIX_TILE = 512
HIST = SUBLANES
STAGE_SLOTS = 2
STAGE_BYTES = 3 << 20

F32 = jnp.float32
BF16 = jnp.bfloat16


def _rms(x, g):
    return x * lax.rsqrt(jnp.mean(x * x, axis=-1, keepdims=True) + EPS) * g


def _resident(shape):
    nd = len(shape)
    return pl.BlockSpec(shape, lambda *_: (0,) * nd, pipeline_mode=pl.Buffered(1))


_HBM = pl.BlockSpec(memory_space=pl.ANY)


def _chunk_rows(k, n):
    best = BF16_ROWS
    for r in range(BF16_ROWS, k + 1, BF16_ROWS):
        if k % r == 0 and r * n * 4 <= STAGE_BYTES:
            best = r
    return best


def _stage_shape(k, n):
    return (STAGE_SLOTS, _chunk_rows(k, n), n)


def _stream_weights_to_bf16(jobs, sems):
    chunks = []
    for src, dst, stage in jobs:
        rows = stage.shape[1]
        for c in range(src.shape[0] // rows):
            chunks.append((src, dst, stage, c * rows, rows))

    def copy(j):
        src, _, stage, r0, rows = chunks[j]
        slot = j % STAGE_SLOTS
        return pltpu.make_async_copy(src.at[pl.ds(r0, rows), :], stage.at[slot], sems.at[slot])

    copy(0).start()
    for j, (_, dst, stage, r0, rows) in enumerate(chunks):
        if j + 1 < len(chunks):
            copy(j + 1).start()
        copy(j).wait()
        dst[r0:r0 + rows, :] = stage[j % STAGE_SLOTS].astype(BF16)


def _dot_nt(a, b_t):
    return lax.dot_general(a, b_t, (((1,), (1,)), ((), ())), preferred_element_type=F32)


def _ffn_rows(x, g_ref, wg_v, wu_v, wd_v, fg_ref, final_norm):
    h = _rms(x, g_ref[...]).astype(BF16)
    gate = _dot_nt(h, wg_v[...])
    up = _dot_nt(h, wu_v[...])
    act = (jax.nn.silu(gate) * up).astype(BF16)
    y = x + 0.5 * jnp.dot(act, wd_v[...], preferred_element_type=F32)
    return _rms(y, fg_ref[...]) if final_norm else y


def _ffn_kernel(xp_ref, xs_ref, g_ref, fg_ref, wg_hbm, wu_hbm, wd_hbm,
                op_ref, os_ref,
                wg_v, wu_v, wd_v, stage, sems, *, n_tiles, final_norm):
    step = pl.program_id(0)

    @pl.when(step == 0)
    def _():
        _stream_weights_to_bf16([(wg_hbm, wg_v, stage), (wu_hbm, wu_v, stage), (wd_hbm, wd_v, stage)], sems)

    @pl.when(step < n_tiles)
    def _():
        op_ref[...] = _ffn_rows(xp_ref[...], g_ref, wg_v, wu_v, wd_v, fg_ref, final_norm)

    @pl.when(step == n_tiles)
    def _():
        os_ref[...] = _ffn_rows(xs_ref[...], g_ref, wg_v, wu_v, wd_v, fg_ref, final_norm)


def _ffn(xp2d, xs2d, g, fg, wg_t, wu_t, wd, *, final_norm):
    m, d = xp2d.shape
    ms = xs2d.shape[0]
    dff = wd.shape[0]
    tm = FFN_TILE
    n_tiles = m // tm
    prompt_rows = pl.BlockSpec((tm, d), lambda i: (jnp.minimum(i, n_tiles - 1), 0))
    sample_rows = pl.BlockSpec((ms, d), lambda i: (0, 0))
    stage = _stage_shape(dff, d)
    vmem = (2 * 3 * d * dff
            + 4 * stage[0] * stage[1] * stage[2]
            + 4 * 4 * (tm + ms) * d
            + tm * (2 * 4 + 2 + 2) * dff + 4 * 4 * tm * d)
    return pl.pallas_call(
        functools.partial(_ffn_kernel, n_tiles=n_tiles, final_norm=final_norm),
        grid=(n_tiles + 1,),
        in_specs=[prompt_rows, sample_rows, _resident(g.shape), _resident(fg.shape), _HBM, _HBM, _HBM],
        out_specs=[prompt_rows, sample_rows],
        out_shape=[jax.ShapeDtypeStruct((m, d), F32), jax.ShapeDtypeStruct((ms, d), F32)],
        scratch_shapes=[pltpu.VMEM((dff, d), BF16)] * 3
                       + [pltpu.VMEM(stage, F32), pltpu.SemaphoreType.DMA((STAGE_SLOTS,))],
        compiler_params=pltpu.CompilerParams(dimension_semantics=("arbitrary",), vmem_limit_bytes=vmem),
    )(xp2d, xs2d, g, fg, wg_t, wu_t, wd)


def _in_proj(h, win_v, k):
    return jnp.dot(h, win_v[:, k * D_MODEL:(k + 1) * D_MODEL], preferred_element_type=F32)


def _lru_coeffs(xc, wr_ref, br_ref, wi_ref, bi_ref, lam_ref):
    xcb = xc.astype(BF16)
    r_parts, i_parts = [], []
    for j in range(GATE_BLOCKS):
        blk = xcb[:, j * MXU_DIM:(j + 1) * MXU_DIM]
        r_parts.append(jnp.dot(blk, wr_ref[j], preferred_element_type=F32))
        i_parts.append(jnp.dot(blk, wi_ref[j], preferred_element_type=F32))
    r = jax.nn.sigmoid(jnp.concatenate(r_parts, axis=-1) + br_ref[...])
    i = jax.nn.sigmoid(jnp.concatenate(i_parts, axis=-1) + bi_ref[...])
    lam = lam_ref[...]
    softplus_neg_lam = jnp.maximum(-lam, 0.0) + jnp.log1p(jnp.exp(-jnp.abs(lam)))
    log_a = -LRU_C * r * softplus_neg_lam
    a = jnp.exp(log_a)
    th = jnp.tanh(log_a)
    b = jnp.sqrt(-2.0 * th / (1.0 - th)) * (i * xc)
    return a, b


def _merge(x, ya_b16, yb, g_a, g_b, pa_v, pb_v, wo_v):
    m = (jax.nn.sigmoid(g_a) * jnp.dot(ya_b16, pa_v[...], preferred_element_type=F32)
         + jax.nn.sigmoid(g_b) * jnp.dot(yb.astype(BF16), pb_v[...], preferred_element_type=F32))
    return x + jnp.dot(m.astype(BF16), wo_v[...], preferred_element_type=F32)


def _mixer_prompt_tile(x_ref, p, win_v, pa_v, pb_v, wo_v, xo_ref, hlast_ref, convnew_ref,
                       xb_scr, a_scr, b_scr, ya_scr, h_scr, first_tile):
    tm = x_ref.shape[0]

    @pl.when(first_tile)
    def _():
        xb_scr[0:HIST, :] = jnp.zeros((HIST, D_MODEL), F32)
        h_scr[...] = jnp.zeros((1, D_MODEL), F32)

    x = x_ref[...]
    h = _rms(x, p["mix_norm"][...]).astype(BF16)

    u = jax.nn.gelu(_in_proj(h, win_v, 0))
    v = _rms(jax.nn.gelu(_in_proj(h, win_v, 1)), p["v_norm"][...]).astype(BF16)
    causal = (lax.broadcasted_iota(jnp.int32, (CHUNK, CHUNK), 0)
              >= lax.broadcasted_iota(jnp.int32, (CHUNK, CHUNK), 1))
    for g in range(A_GROUPS):
        w_g = jnp.where(causal, p["spatial_w"][g], 0.0).astype(BF16)
        bias_g = p["spatial_b_t"][:, g:g + 1]
        cols = slice(g * A_GROUP_DIM, (g + 1) * A_GROUP_DIM)
        for c in range(tm // CHUNK):
            rows = slice(c * CHUNK, (c + 1) * CHUNK)
            s = jnp.dot(w_g, v[rows, cols], preferred_element_type=F32) + bias_g
            ya_scr[rows, cols] = (u[rows, cols] * s).astype(BF16)

    xb = _in_proj(h, win_v, 2)
    xb_scr[HIST:HIST + tm, :] = xb
    cw = p["conv_w"]
    xc = p["conv_b"][...] + cw[CONV_WIDTH - 1:CONV_WIDTH, :] * xb
    for k in range(CONV_WIDTH - 1):
        back = CONV_WIDTH - 1 - k
        xc = xc + cw[k:k + 1, :] * xb_scr[HIST - back:HIST - back + tm, :]
    convnew_ref[...] = xb_scr[HIST + tm - (CONV_WIDTH - 1):HIST + tm, :]
    xb_scr[0:HIST, :] = xb_scr[tm:tm + HIST, :]

    a, b = _lru_coeffs(xc, p["w_r"], p["b_r"], p["w_i"], p["b_i"], p["lam"])
    a_scr[...] = a
    b_scr[...] = b

    first_row = lax.broadcasted_iota(jnp.int32, (SUBLANES, D_MODEL), 0) == 0
    carry = h_scr[...]
    for r0 in range(0, tm, SUBLANES):
        ag = a_scr[r0:r0 + SUBLANES, :]
        hg = jnp.where(first_row, ag, 0.0) * carry + b_scr[r0:r0 + SUBLANES, :]
        span = jnp.where(first_row, 0.0, ag)
        d = 1
        while d < SUBLANES:
            hg = span * pltpu.roll(hg, d, 0) + hg
            if 2 * d < SUBLANES:
                span = span * pltpu.roll(span, d, 0)
            d *= 2
        b_scr[r0:r0 + SUBLANES, :] = hg
        carry = hg[SUBLANES - 1:SUBLANES, :]
    h_scr[...] = carry
    hlast_ref[...] = carry

    yb = b_scr[...] * jax.nn.gelu(_in_proj(h, win_v, 3))
    xo_ref[...] = _merge(x, ya_scr[...], yb, _in_proj(h, win_v, 4), _in_proj(h, win_v, 5),
                         pa_v, pb_v, wo_v)


def _mixer_sample_rows(x_ref, h0_ref, cprev_ref, p, win_v, pa_v, pb_v, wo_v,
                       xo_ref, hnew_ref, xbnew_ref, v_ref):
    x = x_ref[...]
    h = _rms(x, p["mix_norm"][...]).astype(BF16)

    u = jax.nn.gelu(_in_proj(h, win_v, 0))
    v = _rms(jax.nn.gelu(_in_proj(h, win_v, 1)), p["v_norm"][...])
    v_ref[...] = v
    ya = u * (p["spatial_w0"][...] * v + p["spatial_b0"][...])

    xb = _in_proj(h, win_v, 2)
    xbnew_ref[...] = xb
    cw = p["conv_w"]
    xc = p["conv_b"][...] + cw[CONV_WIDTH - 1:CONV_WIDTH, :] * xb
    for k in range(CONV_WIDTH - 1):
        xc = xc + cw[k:k + 1, :] * cprev_ref[k]

    a, b = _lru_coeffs(xc, p["w_r"], p["b_r"], p["w_i"], p["b_i"], p["lam"])
    h_new = a * h0_ref[...] + b
    hnew_ref[...] = h_new

    yb = h_new * jax.nn.gelu(_in_proj(h, win_v, 3))
    xo_ref[...] = _merge(x, ya.astype(BF16), yb, _in_proj(h, win_v, 4), _in_proj(h, win_v, 5),
                         pa_v, pb_v, wo_v)


_SMALL_PARAMS = ("mix_norm", "v_norm", "spatial_w", "spatial_b_t", "spatial_w0", "spatial_b0",
                 "conv_w", "conv_b", "w_r", "b_r", "w_i", "b_i", "lam")


def _mixer_kernel(*refs, n_tiles, tiles_per_seq):
    n_small = len(_SMALL_PARAMS)
    xp_ref, xs_ref, h0_ref, cprev_ref = refs[:4]
    p = dict(zip(_SMALL_PARAMS, refs[4:4 + n_small]))
    win_hbm, pa_hbm, pb_hbm, wo_hbm = refs[4 + n_small:8 + n_small]
    xo_p, hlast_ref, convnew_ref, xo_s, hnew_ref, xbnew_ref, v_ref = refs[8 + n_small:15 + n_small]
    (win_v, pa_v, pb_v, wo_v, stage_in, stage_sq, sems,
     xb_scr, a_scr, b_scr, ya_scr, h_scr) = refs[15 + n_small:]
    step = pl.program_id(0)

    @pl.when(step == 0)
    def _():
        _stream_weights_to_bf16([(win_hbm, win_v, stage_in), (pa_hbm, pa_v, stage_sq),
                                 (pb_hbm, pb_v, stage_sq), (wo_hbm, wo_v, stage_sq)], sems)

    @pl.when(step < n_tiles)
    def _():
        _mixer_prompt_tile(xp_ref, p, win_v, pa_v, pb_v, wo_v, xo_p, hlast_ref, convnew_ref,
                           xb_scr, a_scr, b_scr, ya_scr, h_scr, step % tiles_per_seq == 0)

    @pl.when(step == n_tiles)
    def _():
        _mixer_sample_rows(xs_ref, h0_ref, cprev_ref, p, win_v, pa_v, pb_v, wo_v,
                           xo_s, hnew_ref, xbnew_ref, v_ref)


def _mixer(xp, xs, h0, cprev, small, win, pa, pb, wo):
    bsz, seq, d = xp.shape
    ms = xs.shape[0]
    tm = MIX_TILE
    tiles_per_seq = seq // tm
    n_tiles = bsz * tiles_per_seq
    win_cols = win.shape[1]

    def tile_of(i):
        j = jnp.minimum(i, n_tiles - 1)
        return j // tiles_per_seq, j % tiles_per_seq

    prompt_rows = pl.BlockSpec((None, tm, d), lambda i: (*tile_of(i), 0))
    per_seq = lambda rows: pl.BlockSpec((None, rows, d), lambda i: (tile_of(i)[0], 0, 0))
    sample_rows = pl.BlockSpec((ms, d), lambda i: (0, 0))
    small_args = [small[k] for k in _SMALL_PARAMS]
    stage_in, stage_sq = _stage_shape(d, win_cols), _stage_shape(d, d)
    vmem = (2 * d * (win_cols + 3 * d)
            + 4 * (stage_in[0] * stage_in[1] * stage_in[2] + stage_sq[0] * stage_sq[1] * stage_sq[2])
            + sum(a.size * a.dtype.itemsize for a in small_args) * 2
            + 4 * 4 * tm * d + 4 * 16 * ms * d
            + (4 * 3 + 2) * tm * d + 4 * HIST * d
            + 4 * 4 * tm * d)
    return pl.pallas_call(
        functools.partial(_mixer_kernel, n_tiles=n_tiles, tiles_per_seq=tiles_per_seq),
        grid=(n_tiles + 1,),
        in_specs=[prompt_rows, sample_rows, sample_rows, _resident(cprev.shape)]
                 + [_resident(a.shape) for a in small_args] + [_HBM] * 4,
        out_specs=[prompt_rows, per_seq(1), per_seq(CONV_WIDTH - 1)] + [sample_rows] * 4,
        out_shape=[jax.ShapeDtypeStruct((bsz, seq, d), F32),
                   jax.ShapeDtypeStruct((bsz, 1, d), F32),
                   jax.ShapeDtypeStruct((bsz, CONV_WIDTH - 1, d), F32)]
                  + [jax.ShapeDtypeStruct((ms, d), F32)] * 4,
        scratch_shapes=[pltpu.VMEM((d, win_cols), BF16), pltpu.VMEM((d, d), BF16),
                        pltpu.VMEM((d, d), BF16), pltpu.VMEM((d, d), BF16),
                        pltpu.VMEM(stage_in, F32), pltpu.VMEM(stage_sq, F32),
                        pltpu.SemaphoreType.DMA((STAGE_SLOTS,)),
                        pltpu.VMEM((HIST + tm, d), F32),
                        pltpu.VMEM((tm, d), F32),
                        pltpu.VMEM((tm, d), F32),
                        pltpu.VMEM((tm, d), BF16),
                        pltpu.VMEM((1, d), F32)],
        compiler_params=pltpu.CompilerParams(dimension_semantics=("arbitrary",), vmem_limit_bytes=vmem),
    )(xp, xs, h0, cprev, *small_args, win, pa, pb, wo)


def _block_diag_tiles(w):
    w = w.reshape(GATE_BLOCKS, HEADS_PER_BLOCK, B_HEAD_DIM, B_HEAD_DIM)
    eye = jnp.eye(HEADS_PER_BLOCK, dtype=w.dtype)
    t = jnp.einsum('bhij,hk->bhikj', w, eye)
    return t.reshape(GATE_BLOCKS, MXU_DIM, MXU_DIM)


def kernel(x_prompt, x_sample, state_lru_h, state_conv, ffn1_norm, ffn1_w_gate, ffn1_w_up, ffn1_w_down, mix_norm, w_in, gmlp_v_norm, spatial_w, spatial_b, conv_w, conv_b, lru_w_r, lru_b_r, lru_w_i, lru_b_i, lru_lambda, proj_a, proj_b, w_out, ffn2_norm, ffn2_w_gate, ffn2_w_up, ffn2_w_down, final_norm):
    depth = ffn1_norm.shape[0]
    bsz, seq, d = x_prompt.shape
    nsamp = x_sample.shape[0]
    row = lambda p: p.reshape(1, -1)
    fin = row(final_norm)

    xp = x_prompt
    xs = x_sample.reshape(nsamp, d)
    h_p, c_p, h_s, c_s, v_s = [], [], [], [], []
    for l in range(depth):
        small = {
            "mix_norm": row(mix_norm[l]), "v_norm": row(gmlp_v_norm[l]),
            "spatial_w": spatial_w[l], "spatial_b_t": spatial_b[l].T,
            "spatial_w0": row(jnp.repeat(spatial_w[l][:, 0, 0], A_GROUP_DIM)),
            "spatial_b0": row(jnp.repeat(spatial_b[l][:, 0], A_GROUP_DIM)),
            "conv_w": conv_w[l], "conv_b": row(conv_b[l]),
            "w_r": _block_diag_tiles(lru_w_r[l]).astype(BF16), "b_r": row(lru_b_r[l]),
            "w_i": _block_diag_tiles(lru_w_i[l]).astype(BF16), "b_i": row(lru_b_i[l]),
            "lam": row(lru_lambda[l]),
        }
        last = l == depth - 1

        xp2d, xs = _ffn(xp.reshape(bsz * seq, d), xs, row(ffn1_norm[l]), fin,
                        ffn1_w_gate[l].T, ffn1_w_up[l].T, ffn1_w_down[l], final_norm=False)
        cprev = jnp.swapaxes(state_conv[l], 0, 1)
        xp, hl, cn, xs, hn, xb_new, v_new = _mixer(
            xp2d.reshape(bsz, seq, d), xs, state_lru_h[l], cprev, small,
            w_in[l], proj_a[l], proj_b[l], w_out[l])
        xp2d, xs = _ffn(xp.reshape(bsz * seq, d), xs, row(ffn2_norm[l]), fin,
                        ffn2_w_gate[l].T, ffn2_w_up[l].T, ffn2_w_down[l], final_norm=last)
        xp = xp2d.reshape(bsz, seq, d)

        h_p.append(hl.reshape(bsz, d))
        c_p.append(cn)
        h_s.append(hn)
        c_s.append(jnp.concatenate([state_conv[l][:, 1:], xb_new[:, None, :]], axis=1))
        v_s.append(v_new.reshape(nsamp, 1, d))

    return (xp, xs.reshape(nsamp, 1, d), jnp.stack(h_p), jnp.stack(c_p),
            jnp.stack(h_s), jnp.stack(c_s), jnp.stack(v_s))
```

```python
import functools

import jax
import jax.numpy as jnp
from jax import lax
from jax.experimental import pallas as pl
from jax.experimental.pallas import tpu as pltpu

D_MODEL = 1024
CHUNK = 128
A_GROUPS = 8
A_GROUP_DIM = D_MODEL // A_GROUPS
B_HEADS = 16
B_HEAD_DIM = D_MODEL // B_HEADS
CONV_WIDTH = 4
LRU_C = 8.0
EPS = 1e-6

SUBLANES = 8
LANES = 128
BF16_ROWS = 16
MXU_DIM = 256
GATE_BLOCKS = D_MODEL // MXU_DIM
HEADS_PER_BLOCK = MXU_DIM // B_HEAD_DIM

FFN_TILE = 512
MIX_TILE = 256
PROJ_CHUNK = 2 * MXU_DIM
HIST = SUBLANES
STAGE_SLOTS = 2
STAGE_BYTES = 3 << 20

F32 = jnp.float32
BF16 = jnp.bfloat16


def _rms(x, g):
    return x * lax.rsqrt(jnp.mean(x * x, axis=-1, keepdims=True) + EPS) * g


def _sigmoid(x):
    return 0.5 * jnp.tanh(0.5 * x) + 0.5


def _resident(shape):
    nd = len(shape)
    return pl.BlockSpec(shape, lambda *_: (0,) * nd, pipeline_mode=pl.Buffered(1))


_HBM = pl.BlockSpec(memory_space=pl.ANY)


def _chunk_rows(k, n, max_rows):
    best = BF16_ROWS
    for r in range(BF16_ROWS, min(k, max_rows) + 1, BF16_ROWS):
        if k % r == 0 and r * n * 4 <= STAGE_BYTES:
            best = r
    return best


def _stream_weights_to_bf16(jobs, stages, sems):
    chunks = []
    for src, dst in jobs:
        k, n = src.shape
        rows = _chunk_rows(k, n, stages[0].shape[0])
        chunks.extend((src, dst, c * rows, rows) for c in range(k // rows))

    def window(j):
        src, _, _, rows = chunks[j]
        return stages[j % STAGE_SLOTS].at[pl.ds(0, rows), pl.ds(0, src.shape[1])]

    def copy(j):
        src, _, r0, rows = chunks[j]
        return pltpu.make_async_copy(src.at[pl.ds(r0, rows), :], window(j), sems.at[j % STAGE_SLOTS])

    copy(0).start()
    for j, (_, dst, r0, rows) in enumerate(chunks):
        if j + 1 < len(chunks):
            copy(j + 1).start()
        copy(j).wait()
        dst[r0:r0 + rows, :] = window(j)[...].astype(BF16)


def _dot_nt(a, b_t):
    return lax.dot_general(a, b_t, (((1,), (1,)), ((), ())), preferred_element_type=F32)


def _ffn_rows(x, g_ref, wg_v, wu_v, wd_v, fg_ref, final_norm):
    h = _rms(x, g_ref[...]).astype(BF16)
    gate = _dot_nt(h, wg_v[...])
    up = _dot_nt(h, wu_v[...])
    act = (gate * _sigmoid(gate) * up).astype(BF16)
    y = x + 0.5 * jnp.dot(act, wd_v[...], preferred_element_type=F32)
    return _rms(y, fg_ref[...]) if final_norm else y


def _ffn_kernel(xp_ref, xs_ref, g_ref, fg_ref, wg_hbm, wu_hbm, wd_hbm,
                op_ref, os_ref,
                wg_v, wu_v, wd_v, stage, sems, *, n_tiles, final_norm):
    step = pl.program_id(0)

    @pl.when(step == 0)
    def _():
        _stream_weights_to_bf16([(wg_hbm, wg_v), (wu_hbm, wu_v), (wd_hbm, wd_v)],
                                [stage.at[s] for s in range(STAGE_SLOTS)], sems)

    @pl.when(step < n_tiles)
    def _():
        op_ref[...] = _ffn_rows(xp_ref[...], g_ref, wg_v, wu_v, wd_v, fg_ref, final_norm)

    @pl.when(step == n_tiles)
    def _():
        os_ref[...] = _ffn_rows(xs_ref[...], g_ref, wg_v, wu_v, wd_v, fg_ref, final_norm)


def _ffn(xp2d, xs2d, g, fg, wg_t, wu_t, wd, *, final_norm):
    m, d = xp2d.shape
    ms = xs2d.shape[0]
    dff = wd.shape[0]
    tm = FFN_TILE
    n_tiles = m // tm
    prompt_rows = pl.BlockSpec((tm, d), lambda i: (jnp.minimum(i, n_tiles - 1), 0))
    sample_rows = pl.BlockSpec((ms, d), lambda i: (0, 0))
    stage = (STAGE_SLOTS, _chunk_rows(dff, d, dff), d)
    vmem = (2 * 3 * d * dff
            + 4 * stage[0] * stage[1] * stage[2]
            + 4 * 4 * (tm + ms) * d
            + tm * (2 * 4 + 2 + 2) * dff + 4 * 4 * tm * d)
    return pl.pallas_call(
        functools.partial(_ffn_kernel, n_tiles=n_tiles, final_norm=final_norm),
        grid=(n_tiles + 1,),
        in_specs=[prompt_rows, sample_rows, _resident(g.shape), _resident(fg.shape), _HBM, _HBM, _HBM],
        out_specs=[prompt_rows, sample_rows],
        out_shape=[jax.ShapeDtypeStruct((m, d), F32), jax.ShapeDtypeStruct((ms, d), F32)],
        scratch_shapes=[pltpu.VMEM((dff, d), BF16)] * 3
                       + [pltpu.VMEM(stage, F32), pltpu.SemaphoreType.DMA((STAGE_SLOTS,))],
        compiler_params=pltpu.CompilerParams(dimension_semantics=("arbitrary",), vmem_limit_bytes=vmem),
    )(xp2d, xs2d, g, fg, wg_t, wu_t, wd)


def _gate_block(xcb, w_ref, j):
    return jnp.dot(xcb[:, j * MXU_DIM:(j + 1) * MXU_DIM], w_ref[j], preferred_element_type=F32)


def _lru_coeffs(r_pre, i_pre, xc, b_r, b_i, lam):
    r = _sigmoid(r_pre + b_r)
    i = _sigmoid(i_pre + b_i)
    softplus_neg_lam = jnp.maximum(-lam, 0.0) + jnp.log1p(jnp.exp(-jnp.abs(lam)))
    log_a = -LRU_C * r * softplus_neg_lam
    a = jnp.exp(log_a)
    th = jnp.tanh(log_a)
    b = jnp.sqrt(-2.0 * th / (1.0 - th)) * (i * xc)
    return a, b


def _merge(x, ya_b16, yb, g_a, g_b, pa_v, pb_v, wo_v):
    m = (_sigmoid(g_a) * jnp.dot(ya_b16, pa_v[...], preferred_element_type=F32)
         + _sigmoid(g_b) * jnp.dot(yb.astype(BF16), pb_v[...], preferred_element_type=F32))
    return x + jnp.dot(m.astype(BF16), wo_v[...], preferred_element_type=F32)


def _project_steps(x_ref, p, win_v, hn_scr, z_fill, v_fill, xcb_fill, tail_fill, xb_scr):
    tm = x_ref.shape[0]

    def normalise():
        hn_scr[...] = _rms(x_ref[...], p["mix_norm"][...]).astype(BF16)

    def chunk(c):
        def run():
            z_fill[:, c:c + PROJ_CHUNK] = jnp.dot(hn_scr[...], win_v[:, c:c + PROJ_CHUNK],
                                                  preferred_element_type=F32)
            return z_fill[tm - SUBLANES:tm, c + PROJ_CHUNK - LANES:c + PROJ_CHUNK]
        return run

    def finish_v():
        v_fill[...] = _rms(jax.nn.gelu(z_fill[:, D_MODEL:2 * D_MODEL]), p["v_norm"][...]).astype(BF16)

    def finish_conv():
        xb = z_fill[:, 2 * D_MODEL:3 * D_MODEL]
        xb_scr[HIST:HIST + tm, :] = xb
        cw = p["conv_w"]
        xc = p["conv_b"][...] + cw[CONV_WIDTH - 1:CONV_WIDTH, :] * xb
        for k in range(CONV_WIDTH - 1):
            back = CONV_WIDTH - 1 - k
            xc = xc + cw[k:k + 1, :] * xb_scr[HIST - back:HIST - back + tm, :]
        z_fill[:, 2 * D_MODEL:3 * D_MODEL] = xc
        xcb_fill[...] = xc.astype(BF16)
        tail_fill[...] = xb_scr[tm:tm + HIST, :]
        xb_scr[0:HIST, :] = xb_scr[tm:tm + HIST, :]

    def segment(k):
        return [chunk(c) for c in range(k * D_MODEL, (k + 1) * D_MODEL, PROJ_CHUNK)]

    return ([normalise] + segment(1) + [finish_v] + segment(2) + [finish_conv]
            + segment(0) + segment(3) + segment(4) + segment(5))


def _ordered_after(x, tiles, zero_bits):
    if not tiles:
        return x
    nothing = zero_bits
    for tile in tiles:
        nothing = pltpu.bitcast(tile[:x.shape[0]], jnp.uint32) & nothing
    everywhere = jnp.concatenate([nothing] * (x.shape[1] // LANES), axis=1)
    return pltpu.bitcast(pltpu.bitcast(x, jnp.uint32) | everywhere, F32)


def _mix_tile(x_ref, z, v_ref, xcb_ref, tail_ref, p, pa_v, pb_v, wo_v, xo_ref, hlast_ref, convnew_ref,
              s_scr, a_scr, b_scr, ya_scr, yb_scr, h_scr, overlap):
    tm = x_ref.shape[0]
    seg = lambda k: z[:, k * D_MODEL:(k + 1) * D_MODEL]
    convnew_ref[...] = tail_ref[HIST - (CONV_WIDTH - 1):HIST, :]

    causal = (lax.broadcasted_iota(jnp.int32, (CHUNK, CHUNK), 0)
              >= lax.broadcasted_iota(jnp.int32, (CHUNK, CHUNK), 1))
    blocks = [(slice(c * CHUNK, (c + 1) * CHUNK), slice(g * A_GROUP_DIM, (g + 1) * A_GROUP_DIM), g)
              for g in range(A_GROUPS) for c in range(tm // CHUNK)]
    w_s = [jnp.where(causal, p["spatial_w"][g], 0.0).astype(BF16) for g in range(A_GROUPS)]
    for rows, cols, g in blocks:
        s_scr[rows, cols] = jnp.dot(w_s[g], v_ref[rows, cols], preferred_element_type=F32)
    for j in range(GATE_BLOCKS):
        cols = slice(j * MXU_DIM, (j + 1) * MXU_DIM)
        a_scr[:, cols] = jnp.dot(xcb_ref[:, cols], p["w_r"][j], preferred_element_type=F32)
        b_scr[:, cols] = jnp.dot(xcb_ref[:, cols], p["w_i"][j], preferred_element_type=F32)
    due = overlap(3)

    u = jax.nn.gelu(seg(0))
    for rows, cols, g in blocks:
        bias_g = p["spatial_b_t"][:, g:g + 1]
        ya_scr[rows, cols] = (u[rows, cols] * (s_scr[rows, cols] + bias_g)).astype(BF16)
    due += overlap(3)

    zero_bits = p["zero_bits"][...]
    a, b = _lru_coeffs(a_scr[...], b_scr[...], seg(2), p["b_r"][...], p["b_i"][...], p["lam"][...])
    a_scr[...] = a
    b_scr[...] = b
    due += overlap(3)

    first_row = lax.broadcasted_iota(jnp.int32, (SUBLANES, D_MODEL), 0) == 0
    carry = h_scr[...]
    for r0 in range(0, tm, SUBLANES):
        ag = a_scr[r0:r0 + SUBLANES, :]
        if r0 % (tm // 4) == 0:
            n_wait = len(due) - 2 if r0 == 0 else 1
            ag = _ordered_after(ag, due[:n_wait], zero_bits)
            del due[:n_wait]
        hg = jnp.where(first_row, ag, 0.0) * carry + b_scr[r0:r0 + SUBLANES, :]
        span = jnp.where(first_row, 0.0, ag)
        d = 1
        while d < SUBLANES:
            hg = span * pltpu.roll(hg, d, 0) + hg
            if 2 * d < SUBLANES:
                span = span * pltpu.roll(span, d, 0)
            d *= 2
        b_scr[r0:r0 + SUBLANES, :] = hg
        carry = hg[SUBLANES - 1:SUBLANES, :]
        if (r0 + SUBLANES) % (tm // 4) == 0:
            due += overlap(1)
    h_scr[...] = carry
    hlast_ref[...] = carry

    hl = b_scr[...]
    hl = jnp.concatenate([_ordered_after(hl[:SUBLANES], due[:2], zero_bits), hl[SUBLANES:]], axis=0)
    del due[:2]
    yb_scr[...] = (hl * jax.nn.gelu(seg(3))).astype(BF16)
    due += overlap(2)

    s_scr[...] = jnp.dot(ya_scr[...], pa_v[...], preferred_element_type=F32)
    a_scr[...] = jnp.dot(yb_scr[...], pb_v[...], preferred_element_type=F32)
    m = _sigmoid(seg(4)) * s_scr[...] + _sigmoid(seg(5)) * a_scr[...]
    xo_ref[...] = x_ref[...] + jnp.dot(m.astype(BF16), wo_v[...], preferred_element_type=F32)


def _mixer_sample_rows(x_ref, h0_ref, cprev_ref, p, win_v, pa_v, pb_v, wo_v,
                       xo_ref, hnew_ref, xbnew_ref, v_ref):
    x = x_ref[...]
    h = _rms(x, p["mix_norm"][...]).astype(BF16)
    seg = lambda k: jnp.dot(h, win_v[:, k * D_MODEL:(k + 1) * D_MODEL], preferred_element_type=F32)

    u = jax.nn.gelu(seg(0))
    v = _rms(jax.nn.gelu(seg(1)), p["v_norm"][...])
    v_ref[...] = v
    ya = u * (p["spatial_w0"][...] * v + p["spatial_b0"][...])

    xb = seg(2)
    xbnew_ref[...] = xb
    cw = p["conv_w"]
    xc = p["conv_b"][...] + cw[CONV_WIDTH - 1:CONV_WIDTH, :] * xb
    for k in range(CONV_WIDTH - 1):
        xc = xc + cw[k:k + 1, :] * cprev_ref[k]

    xcb = xc.astype(BF16)
    r_pre = jnp.concatenate([_gate_block(xcb, p["w_r"], j) for j in range(GATE_BLOCKS)], axis=-1)
    i_pre = jnp.concatenate([_gate_block(xcb, p["w_i"], j) for j in range(GATE_BLOCKS)], axis=-1)
    a, b = _lru_coeffs(r_pre, i_pre, xc, p["b_r"][...], p["b_i"][...], p["lam"][...])
    h_new = a * h0_ref[...] + b
    hnew_ref[...] = h_new

    yb = h_new * jax.nn.gelu(seg(3))
    xo_ref[...] = _merge(x, ya.astype(BF16), yb, seg(4), seg(5), pa_v, pb_v, wo_v)


_SMALL_PARAMS = ("mix_norm", "v_norm", "spatial_w", "spatial_b_t", "spatial_w0", "spatial_b0",
                 "conv_w", "conv_b", "w_r", "b_r", "w_i", "b_i", "lam", "zero_bits")


def _mixer_kernel(*refs, n_tiles, tiles_per_seq):
    n_small = len(_SMALL_PARAMS)
    x_next_ref, x_cur_ref, xs_ref, h0_ref, cprev_ref = refs[:5]
    p = dict(zip(_SMALL_PARAMS, refs[5:5 + n_small]))
    win_hbm, pa_hbm, pb_hbm, wo_hbm = refs[5 + n_small:9 + n_small]
    xo_p, hlast_ref, convnew_ref, xo_s, hnew_ref, xbnew_ref, v_ref = refs[9 + n_small:16 + n_small]
    (win_v, pa_v, pb_v, wo_v, z_even, z_odd, v_even, v_odd, xcb_even, xcb_odd, tail_even, tail_odd, sems,
     xb_scr, s_scr, a_scr, b_scr, ya_scr, yb_scr, hn_scr, h_scr) = refs[16 + n_small:]
    step = pl.program_id(0)
    projected_tile = jnp.minimum(step, n_tiles - 1)
    mixed_tile = jnp.maximum(step - 1, 0)

    @pl.when(step == 0)
    def _():
        _stream_weights_to_bf16([(win_hbm, win_v), (pa_hbm, pa_v), (pb_hbm, pb_v), (wo_hbm, wo_v)],
                                (z_even, z_odd), sems)
        z_odd[...] = jnp.zeros(z_odd.shape, F32)
        v_odd[...] = jnp.zeros(v_odd.shape, BF16)
        xcb_odd[...] = jnp.zeros(xcb_odd.shape, BF16)
        tail_odd[...] = jnp.zeros(tail_odd.shape, F32)

    @pl.when(projected_tile % tiles_per_seq == 0)
    def _():
        xb_scr[0:HIST, :] = jnp.zeros((HIST, D_MODEL), F32)

    @pl.when(mixed_tile % tiles_per_seq == 0)
    def _():
        h_scr[...] = jnp.zeros((1, D_MODEL), F32)

    def project_and_mix(fill, drain):
        z_fill, v_fill, xcb_fill, tail_fill = fill
        z_drain, v_drain, xcb_drain, tail_drain = drain
        pending = _project_steps(x_next_ref, p, win_v, hn_scr, z_fill, v_fill, xcb_fill, tail_fill, xb_scr)

        def overlap(n):
            written = [run() for run in pending[:n]]
            del pending[:n]
            return [t for t in written if t is not None]

        _mix_tile(x_cur_ref, z_drain, v_drain, xcb_drain, tail_drain, p, pa_v, pb_v, wo_v,
                  xo_p, hlast_ref, convnew_ref, s_scr, a_scr, b_scr, ya_scr, yb_scr, h_scr, overlap)
        overlap(len(pending))

    even, odd = (z_even, v_even, xcb_even, tail_even), (z_odd, v_odd, xcb_odd, tail_odd)

    @pl.when(step % 2 == 0)
    def _():
        project_and_mix(even, odd)

    @pl.when(step % 2 == 1)
    def _():
        project_and_mix(odd, even)

    @pl.when(step == n_tiles)
    def _():
        _mixer_sample_rows(xs_ref, h0_ref, cprev_ref, p, win_v, pa_v, pb_v, wo_v,
                           xo_s, hnew_ref, xbnew_ref, v_ref)


def _mixer(xp, xs, h0, cprev, small, win, pa, pb, wo):
    bsz, seq, d = xp.shape
    ms = xs.shape[0]
    tm = MIX_TILE
    tiles_per_seq = seq // tm
    n_tiles = bsz * tiles_per_seq
    win_cols = win.shape[1]

    def projected(i):
        j = jnp.minimum(i, n_tiles - 1)
        return j // tiles_per_seq, j % tiles_per_seq

    def mixed(i):
        j = jnp.maximum(i - 1, 0)
        return j // tiles_per_seq, j % tiles_per_seq

    tile = lambda which: pl.BlockSpec((None, tm, d), lambda i: (*which(i), 0))
    per_seq = lambda rows: pl.BlockSpec((None, rows, d), lambda i: (mixed(i)[0], 0, 0))
    sample_rows = pl.BlockSpec((ms, d), lambda i: (0, 0))
    small_args = [small[k] for k in _SMALL_PARAMS]
    vmem = (2 * d * (win_cols + 3 * d)
            + 2 * 4 * tm * win_cols
            + sum(a.size * a.dtype.itemsize for a in small_args) * 2
            + 4 * 6 * tm * d + 4 * 16 * ms * d
            + (4 * 4 + 2 * 7) * tm * d + 4 * 3 * HIST * d
            + 4 * 6 * tm * d)
    return pl.pallas_call(
        functools.partial(_mixer_kernel, n_tiles=n_tiles, tiles_per_seq=tiles_per_seq),
        grid=(n_tiles + 1,),
        in_specs=[tile(projected), tile(mixed), sample_rows, sample_rows, _resident(cprev.shape)]
                 + [_resident(a.shape) for a in small_args] + [_HBM] * 4,
        out_specs=[tile(mixed), per_seq(1), per_seq(CONV_WIDTH - 1)] + [sample_rows] * 4,
        out_shape=[jax.ShapeDtypeStruct((bsz, seq, d), F32),
                   jax.ShapeDtypeStruct((bsz, 1, d), F32),
                   jax.ShapeDtypeStruct((bsz, CONV_WIDTH - 1, d), F32)]
                  + [jax.ShapeDtypeStruct((ms, d), F32)] * 4,
        scratch_shapes=[pltpu.VMEM((d, win_cols), BF16), pltpu.VMEM((d, d), BF16),
                        pltpu.VMEM((d, d), BF16), pltpu.VMEM((d, d), BF16),
                        pltpu.VMEM((tm, win_cols), F32),
                        pltpu.VMEM((tm, win_cols), F32),
                        pltpu.VMEM((tm, d), BF16), pltpu.VMEM((tm, d), BF16),
                        pltpu.VMEM((tm, d), BF16), pltpu.VMEM((tm, d), BF16),
                        pltpu.VMEM((HIST, d), F32), pltpu.VMEM((HIST, d), F32),
                        pltpu.SemaphoreType.DMA((STAGE_SLOTS,)),
                        pltpu.VMEM((HIST + tm, d), F32),
                        pltpu.VMEM((tm, d), F32),
                        pltpu.VMEM((tm, d), F32),
                        pltpu.VMEM((tm, d), F32),
                        pltpu.VMEM((tm, d), BF16),
                        pltpu.VMEM((tm, d), BF16),
                        pltpu.VMEM((tm, d), BF16),
                        pltpu.VMEM((1, d), F32)],
        compiler_params=pltpu.CompilerParams(dimension_semantics=("arbitrary",), vmem_limit_bytes=vmem),
    )(xp, xp, xs, h0, cprev, *small_args, win, pa, pb, wo)


def _block_diag_tiles(w):
    w = w.reshape(GATE_BLOCKS, HEADS_PER_BLOCK, B_HEAD_DIM, B_HEAD_DIM)
    eye = jnp.eye(HEADS_PER_BLOCK, dtype=w.dtype)
    t = jnp.einsum('bhij,hk->bhikj', w, eye)
    return t.reshape(GATE_BLOCKS, MXU_DIM, MXU_DIM)


def kernel(x_prompt, x_sample, state_lru_h, state_conv, ffn1_norm, ffn1_w_gate, ffn1_w_up, ffn1_w_down, mix_norm, w_in, gmlp_v_norm, spatial_w, spatial_b, conv_w, conv_b, lru_w_r, lru_b_r, lru_w_i, lru_b_i, lru_lambda, proj_a, proj_b, w_out, ffn2_norm, ffn2_w_gate, ffn2_w_up, ffn2_w_down, final_norm):
    depth = ffn1_norm.shape[0]
    bsz, seq, d = x_prompt.shape
    nsamp = x_sample.shape[0]
    row = lambda p: p.reshape(1, -1)
    fin = row(final_norm)

    xp = x_prompt
    xs = x_sample.reshape(nsamp, d)
    h_p, c_p, h_s, c_s, v_s = [], [], [], [], []
    for l in range(depth):
        small = {
            "mix_norm": row(mix_norm[l]), "v_norm": row(gmlp_v_norm[l]),
            "spatial_w": spatial_w[l], "spatial_b_t": spatial_b[l].T,
            "spatial_w0": row(jnp.repeat(spatial_w[l][:, 0, 0], A_GROUP_DIM)),
            "spatial_b0": row(jnp.repeat(spatial_b[l][:, 0], A_GROUP_DIM)),
            "conv_w": conv_w[l], "conv_b": row(conv_b[l]),
            "w_r": _block_diag_tiles(lru_w_r[l]).astype(BF16), "b_r": row(lru_b_r[l]),
            "w_i": _block_diag_tiles(lru_w_i[l]).astype(BF16), "b_i": row(lru_b_i[l]),
            "lam": row(lru_lambda[l]),
            "zero_bits": jnp.zeros((SUBLANES, LANES), jnp.uint32),
        }
        last = l == depth - 1

        xp2d, xs = _ffn(xp.reshape(bsz * seq, d), xs, row(ffn1_norm[l]), fin,
                        ffn1_w_gate[l].T, ffn1_w_up[l].T, ffn1_w_down[l], final_norm=False)
        cprev = jnp.swapaxes(state_conv[l], 0, 1)
        xp, hl, cn, xs, hn, xb_new, v_new = _mixer(
            xp2d.reshape(bsz, seq, d), xs, state_lru_h[l], cprev, small,
            w_in[l], proj_a[l], proj_b[l], w_out[l])
        xp2d, xs = _ffn(xp.reshape(bsz * seq, d), xs, row(ffn2_norm[l]), fin,
                        ffn2_w_gate[l].T, ffn2_w_up[l].T, ffn2_w_down[l], final_norm=last)
        xp = xp2d.reshape(bsz, seq, d)

        h_p.append(hl.reshape(bsz, d))
        c_p.append(cn)
        h_s.append(hn)
        c_s.append(jnp.concatenate([state_conv[l][:, 1:], xb_new[:, None, :]], axis=1))
        v_s.append(v_new.reshape(nsamp, 1, d))

    return (xp, xs.reshape(nsamp, 1, d), jnp.stack(h_p), jnp.stack(c_p),
            jnp.stack(h_s), jnp.stack(c_s), jnp.stack(v_s))
```

```python
import functools

import jax
import jax.numpy as jnp
from jax import lax
from jax.experimental import pallas as pl
from jax.experimental.pallas import tpu as pltpu

D_MODEL = 1024
CHUNK = 128
A_GROUPS = 8
A_GROUP_DIM = D_MODEL // A_GROUPS
B_HEADS = 16
B_HEAD_DIM = D_MODEL // B_HEADS
CONV_WIDTH = 4
LRU_C = 8.0
EPS = 1e-6

SUBLANES = 8
BF16_ROWS = 16
MXU_DIM = 256
GATE_BLOCKS = D_MODEL // MXU_DIM
HEADS_PER_BLOCK = MXU_DIM // B_HEAD_DIM

FFN_TILE = 512
MIX_TILE = 512
HIST = SUBLANES
STAGE_SLOTS = 2
STAGE_BYTES = 3 << 20

F32 = jnp.float32
BF16 = jnp.bfloat16


def _rms(x, g):
    return x * lax.rsqrt(jnp.mean(x * x, axis=-1, keepdims=True) + EPS) * g


def _sigmoid(x):
    return 0.5 * jnp.tanh(0.5 * x) + 0.5


def _resident(shape):
    nd = len(shape)
    return pl.BlockSpec(shape, lambda *_: (0,) * nd, pipeline_mode=pl.Buffered(1))


_HBM = pl.BlockSpec(memory_space=pl.ANY)


def _chunk_rows(k, n):
    best = BF16_ROWS
    for r in range(BF16_ROWS, k + 1, BF16_ROWS):
        if k % r == 0 and r * n * 4 <= STAGE_BYTES:
            best = r
    return best


def _stage_shape(k, n):
    return (STAGE_SLOTS, _chunk_rows(k, n), n)


def _stream_weights_to_bf16(jobs, sems):
    chunks = []
    for src, dst, stage in jobs:
        rows = stage.shape[1]
        for c in range(src.shape[0] // rows):
            chunks.append((src, dst, stage, c * rows, rows))

    def copy(j):
        src, _, stage, r0, rows = chunks[j]
        slot = j % STAGE_SLOTS
        return pltpu.make_async_copy(src.at[pl.ds(r0, rows), :], stage.at[slot], sems.at[slot])

    copy(0).start()
    for j, (_, dst, stage, r0, rows) in enumerate(chunks):
        if j + 1 < len(chunks):
            copy(j + 1).start()
        copy(j).wait()
        dst[r0:r0 + rows, :] = stage[j % STAGE_SLOTS].astype(BF16)


def _dot_nt(a, b_t):
    return lax.dot_general(a, b_t, (((1,), (1,)), ((), ())), preferred_element_type=F32)


def _ffn_rows(x, g_ref, wg_v, wu_v, wd_v, fg_ref, final_norm):
    h = _rms(x, g_ref[...]).astype(BF16)
    gate = _dot_nt(h, wg_v[...])
    up = _dot_nt(h, wu_v[...])
    act = (gate * _sigmoid(gate) * up).astype(BF16)
    y = x + 0.5 * jnp.dot(act, wd_v[...], preferred_element_type=F32)
    return _rms(y, fg_ref[...]) if final_norm else y


def _ffn_kernel(xp_ref, xs_ref, g_ref, fg_ref, wg_hbm, wu_hbm, wd_hbm,
                op_ref, os_ref,
                wg_v, wu_v, wd_v, stage, sems, *, n_tiles, final_norm):
    step = pl.program_id(0)

    @pl.when(step == 0)
    def _():
        _stream_weights_to_bf16([(wg_hbm, wg_v, stage), (wu_hbm, wu_v, stage), (wd_hbm, wd_v, stage)], sems)

    @pl.when(step < n_tiles)
    def _():
        op_ref[...] = _ffn_rows(xp_ref[...], g_ref, wg_v, wu_v, wd_v, fg_ref, final_norm)

    @pl.when(step == n_tiles)
    def _():
        os_ref[...] = _ffn_rows(xs_ref[...], g_ref, wg_v, wu_v, wd_v, fg_ref, final_norm)


def _ffn(xp2d, xs2d, g, fg, wg_t, wu_t, wd, *, final_norm):
    m, d = xp2d.shape
    ms = xs2d.shape[0]
    dff = wd.shape[0]
    tm = FFN_TILE
    n_tiles = m // tm
    prompt_rows = pl.BlockSpec((tm, d), lambda i: (jnp.minimum(i, n_tiles - 1), 0))
    sample_rows = pl.BlockSpec((ms, d), lambda i: (0, 0))
    stage = _stage_shape(dff, d)
    vmem = (2 * 3 * d * dff
            + 4 * stage[0] * stage[1] * stage[2]
            + 4 * 4 * (tm + ms) * d
            + tm * (2 * 4 + 2 + 2) * dff + 4 * 4 * tm * d)
    return pl.pallas_call(
        functools.partial(_ffn_kernel, n_tiles=n_tiles, final_norm=final_norm),
        grid=(n_tiles + 1,),
        in_specs=[prompt_rows, sample_rows, _resident(g.shape), _resident(fg.shape), _HBM, _HBM, _HBM],
        out_specs=[prompt_rows, sample_rows],
        out_shape=[jax.ShapeDtypeStruct((m, d), F32), jax.ShapeDtypeStruct((ms, d), F32)],
        scratch_shapes=[pltpu.VMEM((dff, d), BF16)] * 3
                       + [pltpu.VMEM(stage, F32), pltpu.SemaphoreType.DMA((STAGE_SLOTS,))],
        compiler_params=pltpu.CompilerParams(dimension_semantics=("arbitrary",), vmem_limit_bytes=vmem),
    )(xp2d, xs2d, g, fg, wg_t, wu_t, wd)


def _in_proj(h, win_v, k):
    return jnp.dot(h, win_v[:, k * D_MODEL:(k + 1) * D_MODEL], preferred_element_type=F32)


def _lru_coeffs(xc, wr_ref, br_ref, wi_ref, bi_ref, lam_ref):
    xcb = xc.astype(BF16)
    r_parts, i_parts = [], []
    for j in range(GATE_BLOCKS):
        blk = xcb[:, j * MXU_DIM:(j + 1) * MXU_DIM]
        r_parts.append(jnp.dot(blk, wr_ref[j], preferred_element_type=F32))
        i_parts.append(jnp.dot(blk, wi_ref[j], preferred_element_type=F32))
    r = _sigmoid(jnp.concatenate(r_parts, axis=-1) + br_ref[...])
    i = _sigmoid(jnp.concatenate(i_parts, axis=-1) + bi_ref[...])
    lam = lam_ref[...]
    softplus_neg_lam = jnp.maximum(-lam, 0.0) + jnp.log1p(jnp.exp(-jnp.abs(lam)))
    log_a = -LRU_C * r * softplus_neg_lam
    a = jnp.exp(log_a)
    th = jnp.tanh(log_a)
    b = jnp.sqrt(-2.0 * th / (1.0 - th)) * (i * xc)
    return a, b


def _merge(x, ya_b16, yb, g_a, g_b, pa_v, pb_v, wo_v):
    m = (_sigmoid(g_a) * jnp.dot(ya_b16, pa_v[...], preferred_element_type=F32)
         + _sigmoid(g_b) * jnp.dot(yb.astype(BF16), pb_v[...], preferred_element_type=F32))
    return x + jnp.dot(m.astype(BF16), wo_v[...], preferred_element_type=F32)


def _mixer_prompt_tile(x_ref, p, win_v, pa_v, pb_v, wo_v, xo_ref, hlast_ref, convnew_ref,
                       xb_scr, a_scr, b_scr, ya_scr, h_scr, first_tile):
    tm = x_ref.shape[0]

    @pl.when(first_tile)
    def _():
        xb_scr[0:HIST, :] = jnp.zeros((HIST, D_MODEL), F32)
        h_scr[...] = jnp.zeros((1, D_MODEL), F32)

    x = x_ref[...]
    h = _rms(x, p["mix_norm"][...]).astype(BF16)

    u = jax.nn.gelu(_in_proj(h, win_v, 0))
    v = _rms(jax.nn.gelu(_in_proj(h, win_v, 1)), p["v_norm"][...]).astype(BF16)
    causal = (lax.broadcasted_iota(jnp.int32, (CHUNK, CHUNK), 0)
              >= lax.broadcasted_iota(jnp.int32, (CHUNK, CHUNK), 1))
    for g in range(A_GROUPS):
        w_g = jnp.where(causal, p["spatial_w"][g], 0.0).astype(BF16)
        bias_g = p["spatial_b_t"][:, g:g + 1]
        cols = slice(g * A_GROUP_DIM, (g + 1) * A_GROUP_DIM)
        for c in range(tm // CHUNK):
            rows = slice(c * CHUNK, (c + 1) * CHUNK)
            s = jnp.dot(w_g, v[rows, cols], preferred_element_type=F32) + bias_g
            ya_scr[rows, cols] = (u[rows, cols] * s).astype(BF16)

    xb = _in_proj(h, win_v, 2)
    xb_scr[HIST:HIST + tm, :] = xb
    cw = p["conv_w"]
    xc = p["conv_b"][...] + cw[CONV_WIDTH - 1:CONV_WIDTH, :] * xb
    for k in range(CONV_WIDTH - 1):
        back = CONV_WIDTH - 1 - k
        xc = xc + cw[k:k + 1, :] * xb_scr[HIST - back:HIST - back + tm, :]
    convnew_ref[...] = xb_scr[HIST + tm - (CONV_WIDTH - 1):HIST + tm, :]
    xb_scr[0:HIST, :] = xb_scr[tm:tm + HIST, :]

    a, b = _lru_coeffs(xc, p["w_r"], p["b_r"], p["w_i"], p["b_i"], p["lam"])
    a_scr[...] = a
    b_scr[...] = b

    first_row = lax.broadcasted_iota(jnp.int32, (SUBLANES, D_MODEL), 0) == 0
    carry = h_scr[...]
    for r0 in range(0, tm, SUBLANES):
        ag = a_scr[r0:r0 + SUBLANES, :]
        hg = jnp.where(first_row, ag, 0.0) * carry + b_scr[r0:r0 + SUBLANES, :]
        span = jnp.where(first_row, 0.0, ag)
        d = 1
        while d < SUBLANES:
            hg = span * pltpu.roll(hg, d, 0) + hg
            if 2 * d < SUBLANES:
                span = span * pltpu.roll(span, d, 0)
            d *= 2
        b_scr[r0:r0 + SUBLANES, :] = hg
        carry = hg[SUBLANES - 1:SUBLANES, :]
    h_scr[...] = carry
    hlast_ref[...] = carry

    yb = b_scr[...] * jax.nn.gelu(_in_proj(h, win_v, 3))
    xo_ref[...] = _merge(x, ya_scr[...], yb, _in_proj(h, win_v, 4), _in_proj(h, win_v, 5),
                         pa_v, pb_v, wo_v)


def _mixer_sample_rows(x_ref, h0_ref, cprev_ref, p, win_v, pa_v, pb_v, wo_v,
                       xo_ref, hnew_ref, xbnew_ref, v_ref):
    x = x_ref[...]
    h = _rms(x, p["mix_norm"][...]).astype(BF16)

    u = jax.nn.gelu(_in_proj(h, win_v, 0))
    v = _rms(jax.nn.gelu(_in_proj(h, win_v, 1)), p["v_norm"][...])
    v_ref[...] = v
    ya = u * (p["spatial_w0"][...] * v + p["spatial_b0"][...])

    xb = _in_proj(h, win_v, 2)
    xbnew_ref[...] = xb
    cw = p["conv_w"]
    xc = p["conv_b"][...] + cw[CONV_WIDTH - 1:CONV_WIDTH, :] * xb
    for k in range(CONV_WIDTH - 1):
        xc = xc + cw[k:k + 1, :] * cprev_ref[k]

    a, b = _lru_coeffs(xc, p["w_r"], p["b_r"], p["w_i"], p["b_i"], p["lam"])
    h_new = a * h0_ref[...] + b
    hnew_ref[...] = h_new

    yb = h_new * jax.nn.gelu(_in_proj(h, win_v, 3))
    xo_ref[...] = _merge(x, ya.astype(BF16), yb, _in_proj(h, win_v, 4), _in_proj(h, win_v, 5),
                         pa_v, pb_v, wo_v)


_SMALL_PARAMS = ("mix_norm", "v_norm", "spatial_w", "spatial_b_t", "spatial_w0", "spatial_b0",
                 "conv_w", "conv_b", "w_r", "b_r", "w_i", "b_i", "lam")


def _mixer_kernel(*refs, n_tiles, tiles_per_seq):
    n_small = len(_SMALL_PARAMS)
    xp_ref, xs_ref, h0_ref, cprev_ref = refs[:4]
    p = dict(zip(_SMALL_PARAMS, refs[4:4 + n_small]))
    win_hbm, pa_hbm, pb_hbm, wo_hbm = refs[4 + n_small:8 + n_small]
    xo_p, hlast_ref, convnew_ref, xo_s, hnew_ref, xbnew_ref, v_ref = refs[8 + n_small:15 + n_small]
    (win_v, pa_v, pb_v, wo_v, stage_in, stage_sq, sems,
     xb_scr, a_scr, b_scr, ya_scr, h_scr) = refs[15 + n_small:]
    step = pl.program_id(0)

    @pl.when(step == 0)
    def _():
        _stream_weights_to_bf16([(win_hbm, win_v, stage_in), (pa_hbm, pa_v, stage_sq),
                                 (pb_hbm, pb_v, stage_sq), (wo_hbm, wo_v, stage_sq)], sems)

    @pl.when(step < n_tiles)
    def _():
        _mixer_prompt_tile(xp_ref, p, win_v, pa_v, pb_v, wo_v, xo_p, hlast_ref, convnew_ref,
                           xb_scr, a_scr, b_scr, ya_scr, h_scr, step % tiles_per_seq == 0)

    @pl.when(step == n_tiles)
    def _():
        _mixer_sample_rows(xs_ref, h0_ref, cprev_ref, p, win_v, pa_v, pb_v, wo_v,
                           xo_s, hnew_ref, xbnew_ref, v_ref)


def _mixer(xp, xs, h0, cprev, small, win, pa, pb, wo):
    bsz, seq, d = xp.shape
    ms = xs.shape[0]
    tm = MIX_TILE
    tiles_per_seq = seq // tm
    n_tiles = bsz * tiles_per_seq
    win_cols = win.shape[1]

    def tile_of(i):
        j = jnp.minimum(i, n_tiles - 1)
        return j // tiles_per_seq, j % tiles_per_seq

    prompt_rows = pl.BlockSpec((None, tm, d), lambda i: (*tile_of(i), 0))
    per_seq = lambda rows: pl.BlockSpec((None, rows, d), lambda i: (tile_of(i)[0], 0, 0))
    sample_rows = pl.BlockSpec((ms, d), lambda i: (0, 0))
    small_args = [small[k] for k in _SMALL_PARAMS]
    stage_in, stage_sq = _stage_shape(d, win_cols), _stage_shape(d, d)
    vmem = (2 * d * (win_cols + 3 * d)
            + 4 * (stage_in[0] * stage_in[1] * stage_in[2] + stage_sq[0] * stage_sq[1] * stage_sq[2])
            + sum(a.size * a.dtype.itemsize for a in small_args) * 2
            + 4 * 4 * tm * d + 4 * 16 * ms * d
            + (4 * 3 + 2) * tm * d + 4 * HIST * d
            + 4 * 4 * tm * d)
    return pl.pallas_call(
        functools.partial(_mixer_kernel, n_tiles=n_tiles, tiles_per_seq=tiles_per_seq),
        grid=(n_tiles + 1,),
        in_specs=[prompt_rows, sample_rows, sample_rows, _resident(cprev.shape)]
                 + [_resident(a.shape) for a in small_args] + [_HBM] * 4,
        out_specs=[prompt_rows, per_seq(1), per_seq(CONV_WIDTH - 1)] + [sample_rows] * 4,
        out_shape=[jax.ShapeDtypeStruct((bsz, seq, d), F32),
                   jax.ShapeDtypeStruct((bsz, 1, d), F32),
                   jax.ShapeDtypeStruct((bsz, CONV_WIDTH - 1, d), F32)]
                  + [jax.ShapeDtypeStruct((ms, d), F32)] * 4,
        scratch_shapes=[pltpu.VMEM((d, win_cols), BF16), pltpu.VMEM((d, d), BF16),
                        pltpu.VMEM((d, d), BF16), pltpu.VMEM((d, d), BF16),
                        pltpu.VMEM(stage_in, F32), pltpu.VMEM(stage_sq, F32),
                        pltpu.SemaphoreType.DMA((STAGE_SLOTS,)),
                        pltpu.VMEM((HIST + tm, d), F32),
                        pltpu.VMEM((tm, d), F32),
                        pltpu.VMEM((tm, d), F32),
                        pltpu.VMEM((tm, d), BF16),
                        pltpu.VMEM((1, d), F32)],
        compiler_params=pltpu.CompilerParams(dimension_semantics=("arbitrary",), vmem_limit_bytes=vmem),
    )(xp, xs, h0, cprev, *small_args, win, pa, pb, wo)


def _block_diag_tiles(w):
    w = w.reshape(GATE_BLOCKS, HEADS_PER_BLOCK, B_HEAD_DIM, B_HEAD_DIM)
    eye = jnp.eye(HEADS_PER_BLOCK, dtype=w.dtype)
    t = jnp.einsum('bhij,hk->bhikj', w, eye)
    return t.reshape(GATE_BLOCKS, MXU_DIM, MXU_DIM)


def kernel(x_prompt, x_sample, state_lru_h, state_conv, ffn1_norm, ffn1_w_gate, ffn1_w_up, ffn1_w_down, mix_norm, w_in, gmlp_v_norm, spatial_w, spatial_b, conv_w, conv_b, lru_w_r, lru_b_r, lru_w_i, lru_b_i, lru_lambda, proj_a, proj_b, w_out, ffn2_norm, ffn2_w_gate, ffn2_w_up, ffn2_w_down, final_norm):
    depth = ffn1_norm.shape[0]
    bsz, seq, d = x_prompt.shape
    nsamp = x_sample.shape[0]
    row = lambda p: p.reshape(1, -1)
    fin = row(final_norm)

    xp = x_prompt
    xs = x_sample.reshape(nsamp, d)
    h_p, c_p, h_s, c_s, v_s = [], [], [], [], []
    for l in range(depth):
        small = {
            "mix_norm": row(mix_norm[l]), "v_norm": row(gmlp_v_norm[l]),
            "spatial_w": spatial_w[l], "spatial_b_t": spatial_b[l].T,
            "spatial_w0": row(jnp.repeat(spatial_w[l][:, 0, 0], A_GROUP_DIM)),
            "spatial_b0": row(jnp.repeat(spatial_b[l][:, 0], A_GROUP_DIM)),
            "conv_w": conv_w[l], "conv_b": row(conv_b[l]),
            "w_r": _block_diag_tiles(lru_w_r[l]).astype(BF16), "b_r": row(lru_b_r[l]),
            "w_i": _block_diag_tiles(lru_w_i[l]).astype(BF16), "b_i": row(lru_b_i[l]),
            "lam": row(lru_lambda[l]),
        }
        last = l == depth - 1

        xp2d, xs = _ffn(xp.reshape(bsz * seq, d), xs, row(ffn1_norm[l]), fin,
                        ffn1_w_gate[l].T, ffn1_w_up[l].T, ffn1_w_down[l], final_norm=False)
        cprev = jnp.swapaxes(state_conv[l], 0, 1)
        xp, hl, cn, xs, hn, xb_new, v_new = _mixer(
            xp2d.reshape(bsz, seq, d), xs, state_lru_h[l], cprev, small,
            w_in[l], proj_a[l], proj_b[l], w_out[l])
        xp2d, xs = _ffn(xp.reshape(bsz * seq, d), xs, row(ffn2_norm[l]), fin,
                        ffn2_w_gate[l].T, ffn2_w_up[l].T, ffn2_w_down[l], final_norm=last)
        xp = xp2d.reshape(bsz, seq, d)

        h_p.append(hl.reshape(bsz, d))
        c_p.append(cn)
        h_s.append(hn)
        c_s.append(jnp.concatenate([state_conv[l][:, 1:], xb_new[:, None, :]], axis=1))
        v_s.append(v_new.reshape(nsamp, 1, d))

    return (xp, xs.reshape(nsamp, 1, d), jnp.stack(h_p), jnp.stack(c_p),
            jnp.stack(h_s), jnp.stack(c_s), jnp.stack(v_s))
```

```python
import functools

import jax
import jax.numpy as jnp
from jax import lax
from jax.experimental import pallas as pl
from jax.experimental.pallas import tpu as pltpu

D_MODEL = 1024
CHUNK = 128
A_GROUPS = 8
A_GROUP_DIM = D_MODEL // A_GROUPS
B_HEADS = 16
B_HEAD_DIM = D_MODEL // B_HEADS
CONV_WIDTH = 4
LRU_C = 8.0
EPS = 1e-6

SUBLANES = 8
BF16_ROWS = 16
MXU_DIM = 256
GATE_BLOCKS = D_MODEL // MXU_DIM
HEADS_PER_BLOCK = MXU_DIM // B_HEAD_DIM

FFN_TILE = 512
MIX_TILE = 512
HIST = SUBLANES
STAGE_SLOTS = 3
STAGE_BYTES = 3 << 20

F32 = jnp.float32
BF16 = jnp.bfloat16


def _rms(x, g):
    return x * lax.rsqrt(jnp.mean(x * x, axis=-1, keepdims=True) + EPS) * g


def _sigmoid(x):
    return 0.5 * jnp.tanh(0.5 * x) + 0.5


def _resident(shape):
    nd = len(shape)
    return pl.BlockSpec(shape, lambda *_: (0,) * nd, pipeline_mode=pl.Buffered(1))


_HBM = pl.BlockSpec(memory_space=pl.ANY)


def _chunk_rows(k, n):
    best = BF16_ROWS
    for r in range(BF16_ROWS, k + 1, BF16_ROWS):
        if k % r == 0 and r * n * 4 <= STAGE_BYTES:
            best = r
    return best


def _stage_shape(k, n):
    return (STAGE_SLOTS, _chunk_rows(k, n), n)


def _stream_weights_to_bf16(jobs, sems):
    chunks = []
    for src, dst, stage in jobs:
        rows = stage.shape[1]
        for c in range(src.shape[0] // rows):
            chunks.append((src, dst, stage, c * rows, rows))

    def copy(j):
        src, _, stage, r0, rows = chunks[j]
        slot = j % STAGE_SLOTS
        return pltpu.make_async_copy(src.at[pl.ds(r0, rows), :], stage.at[slot], sems.at[slot])

    ahead = STAGE_SLOTS - 1
    for j in range(min(ahead, len(chunks))):
        copy(j).start()
    for j, (_, dst, stage, r0, rows) in enumerate(chunks):
        if j + ahead < len(chunks):
            copy(j + ahead).start()
        copy(j).wait()
        dst[r0:r0 + rows, :] = stage[j % STAGE_SLOTS].astype(BF16)


def _dot_nt(a, b_t):
    return lax.dot_general(a, b_t, (((1,), (1,)), ((), ())), preferred_element_type=F32)


def _ffn_rows(x, g_ref, wg_v, wu_v, wd_v, fg_ref, final_norm):
    h = _rms(x, g_ref[...]).astype(BF16)
    gate = _dot_nt(h, wg_v[...])
    up = _dot_nt(h, wu_v[...])
    act = (gate * _sigmoid(gate) * up).astype(BF16)
    y = x + 0.5 * jnp.dot(act, wd_v[...], preferred_element_type=F32)
    return _rms(y, fg_ref[...]) if final_norm else y


def _ffn_kernel(xp_ref, xs_ref, g_ref, fg_ref, wg_hbm, wu_hbm, wd_hbm,
                op_ref, os_ref,
                wg_v, wu_v, wd_v, stage, sems, *, n_tiles, final_norm):
    step = pl.program_id(0)

    @pl.when(step == 0)
    def _():
        _stream_weights_to_bf16([(wg_hbm, wg_v, stage), (wu_hbm, wu_v, stage), (wd_hbm, wd_v, stage)], sems)

    @pl.when(step < n_tiles)
    def _():
        op_ref[...] = _ffn_rows(xp_ref[...], g_ref, wg_v, wu_v, wd_v, fg_ref, final_norm)

    @pl.when(step == n_tiles)
    def _():
        os_ref[...] = _ffn_rows(xs_ref[...], g_ref, wg_v, wu_v, wd_v, fg_ref, final_norm)


def _ffn(xp2d, xs2d, g, fg, wg_t, wu_t, wd, *, final_norm):
    m, d = xp2d.shape
    ms = xs2d.shape[0]
    dff = wd.shape[0]
    tm = FFN_TILE
    n_tiles = m // tm
    prompt_rows = pl.BlockSpec((tm, d), lambda i: (jnp.minimum(i, n_tiles - 1), 0))
    sample_rows = pl.BlockSpec((ms, d), lambda i: (0, 0))
    stage = _stage_shape(dff, d)
    vmem = (2 * 3 * d * dff
            + 4 * stage[0] * stage[1] * stage[2]
            + 4 * 4 * (tm + ms) * d
            + tm * (2 * 4 + 2 + 2) * dff + 4 * 4 * tm * d)
    return pl.pallas_call(
        functools.partial(_ffn_kernel, n_tiles=n_tiles, final_norm=final_norm),
        grid=(n_tiles + 1,),
        in_specs=[prompt_rows, sample_rows, _resident(g.shape), _resident(fg.shape), _HBM, _HBM, _HBM],
        out_specs=[prompt_rows, sample_rows],
        out_shape=[jax.ShapeDtypeStruct((m, d), F32), jax.ShapeDtypeStruct((ms, d), F32)],
        scratch_shapes=[pltpu.VMEM((dff, d), BF16)] * 3
                       + [pltpu.VMEM(stage, F32), pltpu.SemaphoreType.DMA((STAGE_SLOTS,))],
        compiler_params=pltpu.CompilerParams(dimension_semantics=("arbitrary",), vmem_limit_bytes=vmem),
    )(xp2d, xs2d, g, fg, wg_t, wu_t, wd)


def _in_proj(h, win_v, k):
    return jnp.dot(h, win_v[:, k * D_MODEL:(k + 1) * D_MODEL], preferred_element_type=F32)


def _lru_coeffs(xc, wr_ref, br_ref, wi_ref, bi_ref, lam_ref):
    xcb = xc.astype(BF16)
    r_parts, i_parts = [], []
    for j in range(GATE_BLOCKS):
        blk = xcb[:, j * MXU_DIM:(j + 1) * MXU_DIM]
        r_parts.append(jnp.dot(blk, wr_ref[j], preferred_element_type=F32))
        i_parts.append(jnp.dot(blk, wi_ref[j], preferred_element_type=F32))
    r = _sigmoid(jnp.concatenate(r_parts, axis=-1) + br_ref[...])
    i = _sigmoid(jnp.concatenate(i_parts, axis=-1) + bi_ref[...])
    lam = lam_ref[...]
    softplus_neg_lam = jnp.maximum(-lam, 0.0) + jnp.log1p(jnp.exp(-jnp.abs(lam)))
    log_a = -LRU_C * r * softplus_neg_lam
    a = jnp.exp(log_a)
    th = jnp.tanh(log_a)
    q = -2.0 * th / (1.0 - th)
    root = jnp.where(q == 0.0, 0.0, q * lax.rsqrt(q))
    b = root * (i * xc)
    return a, b


def _merge(x, ya_b16, yb, g_a, g_b, pa_v, pb_v, wo_v):
    m = (_sigmoid(g_a) * jnp.dot(ya_b16, pa_v[...], preferred_element_type=F32)
         + _sigmoid(g_b) * jnp.dot(yb.astype(BF16), pb_v[...], preferred_element_type=F32))
    return x + jnp.dot(m.astype(BF16), wo_v[...], preferred_element_type=F32)


def _mixer_prompt_tile(x_ref, p, win_v, pa_v, pb_v, wo_v, xo_ref, hlast_ref, convnew_ref,
                       xb_scr, a_scr, b_scr, ya_scr, h_scr, first_tile):
    tm = x_ref.shape[0]

    @pl.when(first_tile)
    def _():
        xb_scr[0:HIST, :] = jnp.zeros((HIST, D_MODEL), F32)
        h_scr[...] = jnp.zeros((1, D_MODEL), F32)

    x = x_ref[...]
    h = _rms(x, p["mix_norm"][...]).astype(BF16)

    u = jax.nn.gelu(_in_proj(h, win_v, 0))
    v = _rms(jax.nn.gelu(_in_proj(h, win_v, 1)), p["v_norm"][...]).astype(BF16)
    causal = (lax.broadcasted_iota(jnp.int32, (CHUNK, CHUNK), 0)
              >= lax.broadcasted_iota(jnp.int32, (CHUNK, CHUNK), 1))
    for g in range(A_GROUPS):
        w_g = jnp.where(causal, p["spatial_w"][g], 0.0).astype(BF16)
        bias_g = p["spatial_b_t"][:, g:g + 1]
        cols = slice(g * A_GROUP_DIM, (g + 1) * A_GROUP_DIM)
        for c in range(tm // CHUNK):
            rows = slice(c * CHUNK, (c + 1) * CHUNK)
            s = jnp.dot(w_g, v[rows, cols], preferred_element_type=F32) + bias_g
            ya_scr[rows, cols] = (u[rows, cols] * s).astype(BF16)

    xb = _in_proj(h, win_v, 2)
    xb_scr[HIST:HIST + tm, :] = xb
    cw = p["conv_w"]
    xc = p["conv_b"][...] + cw[CONV_WIDTH - 1:CONV_WIDTH, :] * xb
    for k in range(CONV_WIDTH - 1):
        back = CONV_WIDTH - 1 - k
        xc = xc + cw[k:k + 1, :] * xb_scr[HIST - back:HIST - back + tm, :]
    convnew_ref[...] = xb_scr[HIST + tm - (CONV_WIDTH - 1):HIST + tm, :]
    xb_scr[0:HIST, :] = xb_scr[tm:tm + HIST, :]

    a, b = _lru_coeffs(xc, p["w_r"], p["b_r"], p["w_i"], p["b_i"], p["lam"])
    a_scr[...] = a
    b_scr[...] = b

    first_row = lax.broadcasted_iota(jnp.int32, (SUBLANES, D_MODEL), 0) == 0
    carry = h_scr[...]
    for r0 in range(0, tm, SUBLANES):
        ag = a_scr[r0:r0 + SUBLANES, :]
        hg = jnp.where(first_row, ag, 0.0) * carry + b_scr[r0:r0 + SUBLANES, :]
        span = jnp.where(first_row, 0.0, ag)
        d = 1
        while d < SUBLANES:
            hg = span * pltpu.roll(hg, d, 0) + hg
            if 2 * d < SUBLANES:
                span = span * pltpu.roll(span, d, 0)
            d *= 2
        b_scr[r0:r0 + SUBLANES, :] = hg
        carry = hg[SUBLANES - 1:SUBLANES, :]
    h_scr[...] = carry
    hlast_ref[...] = carry

    yb = b_scr[...] * jax.nn.gelu(_in_proj(h, win_v, 3))
    xo_ref[...] = _merge(x, ya_scr[...], yb, _in_proj(h, win_v, 4), _in_proj(h, win_v, 5),
                         pa_v, pb_v, wo_v)


def _mixer_sample_rows(x_ref, h0_ref, cprev_ref, p, win_v, pa_v, pb_v, wo_v,
                       xo_ref, hnew_ref, xbnew_ref, v_ref):
    x = x_ref[...]
    h = _rms(x, p["mix_norm"][...]).astype(BF16)

    u = jax.nn.gelu(_in_proj(h, win_v, 0))
    v = _rms(jax.nn.gelu(_in_proj(h, win_v, 1)), p["v_norm"][...])
    v_ref[...] = v
    ya = u * (p["spatial_w0"][...] * v + p["spatial_b0"][...])

    xb = _in_proj(h, win_v, 2)
    xbnew_ref[...] = xb
    cw = p["conv_w"]
    xc = p["conv_b"][...] + cw[CONV_WIDTH - 1:CONV_WIDTH, :] * xb
    for k in range(CONV_WIDTH - 1):
        xc = xc + cw[k:k + 1, :] * cprev_ref[k]

    a, b = _lru_coeffs(xc, p["w_r"], p["b_r"], p["w_i"], p["b_i"], p["lam"])
    h_new = a * h0_ref[...] + b
    hnew_ref[...] = h_new

    yb = h_new * jax.nn.gelu(_in_proj(h, win_v, 3))
    xo_ref[...] = _merge(x, ya.astype(BF16), yb, _in_proj(h, win_v, 4), _in_proj(h, win_v, 5),
                         pa_v, pb_v, wo_v)


_SMALL_PARAMS = ("mix_norm", "v_norm", "spatial_w", "spatial_b_t", "spatial_w0", "spatial_b0",
                 "conv_w", "conv_b", "w_r", "b_r", "w_i", "b_i", "lam")


def _mixer_kernel(*refs, n_tiles, tiles_per_seq):
    n_small = len(_SMALL_PARAMS)
    xp_ref, xs_ref, h0_ref, cprev_ref = refs[:4]
    p = dict(zip(_SMALL_PARAMS, refs[4:4 + n_small]))
    win_hbm, pa_hbm, pb_hbm, wo_hbm = refs[4 + n_small:8 + n_small]
    xo_p, hlast_ref, convnew_ref, xo_s, hnew_ref, xbnew_ref, v_ref = refs[8 + n_small:15 + n_small]
    (win_v, pa_v, pb_v, wo_v, stage_in, stage_sq, sems,
     xb_scr, a_scr, b_scr, ya_scr, h_scr) = refs[15 + n_small:]
    step = pl.program_id(0)

    @pl.when(step == 0)
    def _():
        _stream_weights_to_bf16([(win_hbm, win_v, stage_in), (pa_hbm, pa_v, stage_sq),
                                 (pb_hbm, pb_v, stage_sq), (wo_hbm, wo_v, stage_sq)], sems)

    @pl.when(step < n_tiles)
    def _():
        _mixer_prompt_tile(xp_ref, p, win_v, pa_v, pb_v, wo_v, xo_p, hlast_ref, convnew_ref,
                           xb_scr, a_scr, b_scr, ya_scr, h_scr, step % tiles_per_seq == 0)

    @pl.when(step == n_tiles)
    def _():
        _mixer_sample_rows(xs_ref, h0_ref, cprev_ref, p, win_v, pa_v, pb_v, wo_v,
                           xo_s, hnew_ref, xbnew_ref, v_ref)


def _mixer(xp, xs, h0, cprev, small, win, pa, pb, wo):
    bsz, seq, d = xp.shape
    ms = xs.shape[0]
    tm = MIX_TILE
    tiles_per_seq = seq // tm
    n_tiles = bsz * tiles_per_seq
    win_cols = win.shape[1]

    def tile_of(i):
        j = jnp.minimum(i, n_tiles - 1)
        return j // tiles_per_seq, j % tiles_per_seq

    prompt_rows = pl.BlockSpec((None, tm, d), lambda i: (*tile_of(i), 0))
    per_seq = lambda rows: pl.BlockSpec((None, rows, d), lambda i: (tile_of(i)[0], 0, 0))
    sample_rows = pl.BlockSpec((ms, d), lambda i: (0, 0))
    small_args = [small[k] for k in _SMALL_PARAMS]
    stage_in, stage_sq = _stage_shape(d, win_cols), _stage_shape(d, d)
    vmem = (2 * d * (win_cols + 3 * d)
            + 4 * (stage_in[0] * stage_in[1] * stage_in[2] + stage_sq[0] * stage_sq[1] * stage_sq[2])
            + sum(a.size * a.dtype.itemsize for a in small_args) * 2
            + 4 * 4 * tm * d + 4 * 16 * ms * d
            + (4 * 3 + 2) * tm * d + 4 * HIST * d
            + 4 * 4 * tm * d)
    return pl.pallas_call(
        functools.partial(_mixer_kernel, n_tiles=n_tiles, tiles_per_seq=tiles_per_seq),
        grid=(n_tiles + 1,),
        in_specs=[prompt_rows, sample_rows, sample_rows, _resident(cprev.shape)]
                 + [_resident(a.shape) for a in small_args] + [_HBM] * 4,
        out_specs=[prompt_rows, per_seq(1), per_seq(CONV_WIDTH - 1)] + [sample_rows] * 4,
        out_shape=[jax.ShapeDtypeStruct((bsz, seq, d), F32),
                   jax.ShapeDtypeStruct((bsz, 1, d), F32),
                   jax.ShapeDtypeStruct((bsz, CONV_WIDTH - 1, d), F32)]
                  + [jax.ShapeDtypeStruct((ms, d), F32)] * 4,
        scratch_shapes=[pltpu.VMEM((d, win_cols), BF16), pltpu.VMEM((d, d), BF16),
                        pltpu.VMEM((d, d), BF16), pltpu.VMEM((d, d), BF16),
                        pltpu.VMEM(stage_in, F32), pltpu.VMEM(stage_sq, F32),
                        pltpu.SemaphoreType.DMA((STAGE_SLOTS,)),
                        pltpu.VMEM((HIST + tm, d), F32),
                        pltpu.VMEM((tm, d), F32),
                        pltpu.VMEM((tm, d), F32),
                        pltpu.VMEM((tm, d), BF16),
                        pltpu.VMEM((1, d), F32)],
        compiler_params=pltpu.CompilerParams(dimension_semantics=("arbitrary",), vmem_limit_bytes=vmem),
    )(xp, xs, h0, cprev, *small_args, win, pa, pb, wo)


def _block_diag_tiles(w):
    w = w.reshape(GATE_BLOCKS, HEADS_PER_BLOCK, B_HEAD_DIM, B_HEAD_DIM)
    eye = jnp.eye(HEADS_PER_BLOCK, dtype=w.dtype)
    t = jnp.einsum('bhij,hk->bhikj', w, eye)
    return t.reshape(GATE_BLOCKS, MXU_DIM, MXU_DIM)


def kernel(x_prompt, x_sample, state_lru_h, state_conv, ffn1_norm, ffn1_w_gate, ffn1_w_up, ffn1_w_down, mix_norm, w_in, gmlp_v_norm, spatial_w, spatial_b, conv_w, conv_b, lru_w_r, lru_b_r, lru_w_i, lru_b_i, lru_lambda, proj_a, proj_b, w_out, ffn2_norm, ffn2_w_gate, ffn2_w_up, ffn2_w_down, final_norm):
    depth = ffn1_norm.shape[0]
    bsz, seq, d = x_prompt.shape
    nsamp = x_sample.shape[0]
    row = lambda p: p.reshape(1, -1)
    fin = row(final_norm)

    xp = x_prompt
    xs = x_sample.reshape(nsamp, d)
    h_p, c_p, h_s, c_s, v_s = [], [], [], [], []
    for l in range(depth):
        small = {
            "mix_norm": row(mix_norm[l]), "v_norm": row(gmlp_v_norm[l]),
            "spatial_w": spatial_w[l], "spatial_b_t": spatial_b[l].T,
            "spatial_w0": row(jnp.repeat(spatial_w[l][:, 0, 0], A_GROUP_DIM)),
            "spatial_b0": row(jnp.repeat(spatial_b[l][:, 0], A_GROUP_DIM)),
            "conv_w": conv_w[l], "conv_b": row(conv_b[l]),
            "w_r": _block_diag_tiles(lru_w_r[l]).astype(BF16), "b_r": row(lru_b_r[l]),
            "w_i": _block_diag_tiles(lru_w_i[l]).astype(BF16), "b_i": row(lru_b_i[l]),
            "lam": row(lru_lambda[l]),
        }
        last = l == depth - 1

        xp2d, xs = _ffn(xp.reshape(bsz * seq, d), xs, row(ffn1_norm[l]), fin,
                        ffn1_w_gate[l].T, ffn1_w_up[l].T, ffn1_w_down[l], final_norm=False)
        cprev = jnp.swapaxes(state_conv[l], 0, 1)
        xp, hl, cn, xs, hn, xb_new, v_new = _mixer(
            xp2d.reshape(bsz, seq, d), xs, state_lru_h[l], cprev, small,
            w_in[l], proj_a[l], proj_b[l], w_out[l])
        xp2d, xs = _ffn(xp.reshape(bsz * seq, d), xs, row(ffn2_norm[l]), fin,
                        ffn2_w_gate[l].T, ffn2_w_up[l].T, ffn2_w_down[l], final_norm=last)
        xp = xp2d.reshape(bsz, seq, d)

        h_p.append(hl.reshape(bsz, d))
        c_p.append(cn)
        h_s.append(hn)
        c_s.append(jnp.concatenate([state_conv[l][:, 1:], xb_new[:, None, :]], axis=1))
        v_s.append(v_new.reshape(nsamp, 1, d))

    return (xp, xs.reshape(nsamp, 1, d), jnp.stack(h_p), jnp.stack(c_p),
            jnp.stack(h_s), jnp.stack(c_s), jnp.stack(v_s))
```

```python
import functools

import jax
import jax.numpy as jnp
from jax import lax
from jax.experimental import pallas as pl
from jax.experimental.pallas import tpu as pltpu

D_MODEL = 1024
CHUNK = 128
A_GROUPS = 8
A_GROUP_DIM = D_MODEL // A_GROUPS
B_HEADS = 16
B_HEAD_DIM = D_MODEL // B_HEADS
CONV_WIDTH = 4
LRU_C = 8.0
EPS = 1e-6

SUBLANES = 8
BF16_ROWS = 16
MXU_DIM = 256
GATE_BLOCKS = D_MODEL // MXU_DIM
HEADS_PER_BLOCK = MXU_DIM // B_HEAD_DIM

FFN_TILE = 512
MIX_TILE = 256
HIST = SUBLANES
STAGE_SLOTS = 3
STAGE_BYTES = 3 << 20

F32 = jnp.float32
BF16 = jnp.bfloat16


def _rms(x, g):
    return x * lax.rsqrt(jnp.mean(x * x, axis=-1, keepdims=True) + EPS) * g


def _sigmoid(x):
    return 0.5 * jnp.tanh(0.5 * x) + 0.5


def _resident(shape):
    nd = len(shape)
    return pl.BlockSpec(shape, lambda *_: (0,) * nd, pipeline_mode=pl.Buffered(1))


_HBM = pl.BlockSpec(memory_space=pl.ANY)


def _chunk_rows(k, n):
    best = BF16_ROWS
    for r in range(BF16_ROWS, k + 1, BF16_ROWS):
        if k % r == 0 and r * n * 4 <= STAGE_BYTES:
            best = r
    return best


def _stage_shape(k, n):
    return (STAGE_SLOTS, _chunk_rows(k, n), n)


def _stream_weights_to_bf16(jobs, sems):
    chunks = []
    for src, dst, stage in jobs:
        rows = stage.shape[1]
        for c in range(src.shape[0] // rows):
            chunks.append((src, dst, stage, c * rows, rows))

    def copy(j):
        src, _, stage, r0, rows = chunks[j]
        slot = j % STAGE_SLOTS
        return pltpu.make_async_copy(src.at[pl.ds(r0, rows), :], stage.at[slot], sems.at[slot])

    ahead = STAGE_SLOTS - 1
    for j in range(min(ahead, len(chunks))):
        copy(j).start()
    for j, (_, dst, stage, r0, rows) in enumerate(chunks):
        if j + ahead < len(chunks):
            copy(j + ahead).start()
        copy(j).wait()
        dst[r0:r0 + rows, :] = stage[j % STAGE_SLOTS].astype(BF16)


def _dot_nt(a, b_t):
    return lax.dot_general(a, b_t, (((1,), (1,)), ((), ())), preferred_element_type=F32)


def _ffn_rows(x, g_ref, wg_v, wu_v, wd_v, fg_ref, final_norm):
    h = _rms(x, g_ref[...]).astype(BF16)
    gate = _dot_nt(h, wg_v[...])
    up = _dot_nt(h, wu_v[...])
    act = (gate * _sigmoid(gate) * up).astype(BF16)
    y = x + 0.5 * jnp.dot(act, wd_v[...], preferred_element_type=F32)
    return _rms(y, fg_ref[...]) if final_norm else y


def _ffn_kernel(xp_ref, xs_ref, g_ref, fg_ref, wg_hbm, wu_hbm, wd_hbm,
                op_ref, os_ref,
                wg_v, wu_v, wd_v, stage, sems, *, n_tiles, final_norm):
    step = pl.program_id(0)

    @pl.when(step == 0)
    def _():
        _stream_weights_to_bf16([(wg_hbm, wg_v, stage), (wu_hbm, wu_v, stage), (wd_hbm, wd_v, stage)], sems)

    @pl.when(step < n_tiles)
    def _():
        op_ref[...] = _ffn_rows(xp_ref[...], g_ref, wg_v, wu_v, wd_v, fg_ref, final_norm)

    @pl.when(step == n_tiles)
    def _():
        os_ref[...] = _ffn_rows(xs_ref[...], g_ref, wg_v, wu_v, wd_v, fg_ref, final_norm)


def _ffn(xp2d, xs2d, g, fg, wg_t, wu_t, wd, *, final_norm):
    m, d = xp2d.shape
    ms = xs2d.shape[0]
    dff = wd.shape[0]
    tm = FFN_TILE
    n_tiles = m // tm
    prompt_rows = pl.BlockSpec((tm, d), lambda i: (jnp.minimum(i, n_tiles - 1), 0))
    sample_rows = pl.BlockSpec((ms, d), lambda i: (0, 0))
    stage = _stage_shape(dff, d)
    vmem = (2 * 3 * d * dff
            + 4 * stage[0] * stage[1] * stage[2]
            + 4 * 4 * (tm + ms) * d
            + tm * (2 * 4 + 2 + 2) * dff + 4 * 4 * tm * d)
    return pl.pallas_call(
        functools.partial(_ffn_kernel, n_tiles=n_tiles, final_norm=final_norm),
        grid=(n_tiles + 1,),
        in_specs=[prompt_rows, sample_rows, _resident(g.shape), _resident(fg.shape), _HBM, _HBM, _HBM],
        out_specs=[prompt_rows, sample_rows],
        out_shape=[jax.ShapeDtypeStruct((m, d), F32), jax.ShapeDtypeStruct((ms, d), F32)],
        scratch_shapes=[pltpu.VMEM((dff, d), BF16)] * 3
                       + [pltpu.VMEM(stage, F32), pltpu.SemaphoreType.DMA((STAGE_SLOTS,))],
        compiler_params=pltpu.CompilerParams(dimension_semantics=("arbitrary",), vmem_limit_bytes=vmem),
    )(xp2d, xs2d, g, fg, wg_t, wu_t, wd)


def _in_proj(h, win_v, k):
    return jnp.dot(h, win_v[:, k * D_MODEL:(k + 1) * D_MODEL], preferred_element_type=F32)


def _lru_coeffs(xc, wr_ref, br_ref, wi_ref, bi_ref, lam_ref):
    xcb = xc.astype(BF16)
    r_parts, i_parts = [], []
    for j in range(GATE_BLOCKS):
        blk = xcb[:, j * MXU_DIM:(j + 1) * MXU_DIM]
        r_parts.append(jnp.dot(blk, wr_ref[j], preferred_element_type=F32))
        i_parts.append(jnp.dot(blk, wi_ref[j], preferred_element_type=F32))
    r = _sigmoid(jnp.concatenate(r_parts, axis=-1) + br_ref[...])
    i = _sigmoid(jnp.concatenate(i_parts, axis=-1) + bi_ref[...])
    lam = lam_ref[...]
    softplus_neg_lam = jnp.maximum(-lam, 0.0) + jnp.log1p(jnp.exp(-jnp.abs(lam)))
    log_a = -LRU_C * r * softplus_neg_lam
    a = jnp.exp(log_a)
    th = jnp.tanh(log_a)
    q = -2.0 * th / (1.0 - th)
    root = jnp.where(q == 0.0, 0.0, q * lax.rsqrt(q))
    b = root * (i * xc)
    return a, b


def _merge(x, ya_b16, yb, g_a, g_b, pa_v, pb_v, wo_v):
    m = (_sigmoid(g_a) * jnp.dot(ya_b16, pa_v[...], preferred_element_type=F32)
         + _sigmoid(g_b) * jnp.dot(yb.astype(BF16), pb_v[...], preferred_element_type=F32))
    return x + jnp.dot(m.astype(BF16), wo_v[...], preferred_element_type=F32)


def _mixer_prompt_tile(x_ref, p, win_v, pa_v, pb_v, wo_v, xo_ref, hlast_ref, convnew_ref,
                       xb_scr, a_scr, b_scr, ya_scr, h_scr, first_tile):
    tm = x_ref.shape[0]

    @pl.when(first_tile)
    def _():
        xb_scr[0:HIST, :] = jnp.zeros((HIST, D_MODEL), F32)
        h_scr[...] = jnp.zeros((1, D_MODEL), F32)

    x = x_ref[...]
    h = _rms(x, p["mix_norm"][...]).astype(BF16)

    u = jax.nn.gelu(_in_proj(h, win_v, 0))
    v = _rms(jax.nn.gelu(_in_proj(h, win_v, 1)), p["v_norm"][...]).astype(BF16)
    causal = (lax.broadcasted_iota(jnp.int32, (CHUNK, CHUNK), 0)
              >= lax.broadcasted_iota(jnp.int32, (CHUNK, CHUNK), 1))
    for g in range(A_GROUPS):
        w_g = jnp.where(causal, p["spatial_w"][g], 0.0).astype(BF16)
        bias_g = p["spatial_b_t"][:, g:g + 1]
        cols = slice(g * A_GROUP_DIM, (g + 1) * A_GROUP_DIM)
        for c in range(tm // CHUNK):
            rows = slice(c * CHUNK, (c + 1) * CHUNK)
            s = jnp.dot(w_g, v[rows, cols], preferred_element_type=F32) + bias_g
            ya_scr[rows, cols] = (u[rows, cols] * s).astype(BF16)

    xb = _in_proj(h, win_v, 2)
    xb_scr[HIST:HIST + tm, :] = xb
    cw = p["conv_w"]
    xc = p["conv_b"][...] + cw[CONV_WIDTH - 1:CONV_WIDTH, :] * xb
    for k in range(CONV_WIDTH - 1):
        back = CONV_WIDTH - 1 - k
        xc = xc + cw[k:k + 1, :] * xb_scr[HIST - back:HIST - back + tm, :]
    convnew_ref[...] = xb_scr[HIST + tm - (CONV_WIDTH - 1):HIST + tm, :]
    xb_scr[0:HIST, :] = xb_scr[tm:tm + HIST, :]

    a, b = _lru_coeffs(xc, p["w_r"], p["b_r"], p["w_i"], p["b_i"], p["lam"])
    a_scr[...] = a
    b_scr[...] = b

    first_row = lax.broadcasted_iota(jnp.int32, (SUBLANES, D_MODEL), 0) == 0
    carry = h_scr[...]
    for r0 in range(0, tm, SUBLANES):
        ag = a_scr[r0:r0 + SUBLANES, :]
        hg = jnp.where(first_row, ag, 0.0) * carry + b_scr[r0:r0 + SUBLANES, :]
        span = jnp.where(first_row, 0.0, ag)
        d = 1
        while d < SUBLANES:
            hg = span * pltpu.roll(hg, d, 0) + hg
            if 2 * d < SUBLANES:
                span = span * pltpu.roll(span, d, 0)
            d *= 2
        b_scr[r0:r0 + SUBLANES, :] = hg
        carry = hg[SUBLANES - 1:SUBLANES, :]
    h_scr[...] = carry
    hlast_ref[...] = carry

    yb = b_scr[...] * jax.nn.gelu(_in_proj(h, win_v, 3))
    xo_ref[...] = _merge(x, ya_scr[...], yb, _in_proj(h, win_v, 4), _in_proj(h, win_v, 5),
                         pa_v, pb_v, wo_v)


def _mixer_sample_rows(x_ref, h0_ref, cprev_ref, p, win_v, pa_v, pb_v, wo_v,
                       xo_ref, hnew_ref, xbnew_ref, v_ref):
    x = x_ref[...]
    h = _rms(x, p["mix_norm"][...]).astype(BF16)

    u = jax.nn.gelu(_in_proj(h, win_v, 0))
    v = _rms(jax.nn.gelu(_in_proj(h, win_v, 1)), p["v_norm"][...])
    v_ref[...] = v
    ya = u * (p["spatial_w0"][...] * v + p["spatial_b0"][...])

    xb = _in_proj(h, win_v, 2)
    xbnew_ref[...] = xb
    cw = p["conv_w"]
    xc = p["conv_b"][...] + cw[CONV_WIDTH - 1:CONV_WIDTH, :] * xb
    for k in range(CONV_WIDTH - 1):
        xc = xc + cw[k:k + 1, :] * cprev_ref[k]

    a, b = _lru_coeffs(xc, p["w_r"], p["b_r"], p["w_i"], p["b_i"], p["lam"])
    h_new = a * h0_ref[...] + b
    hnew_ref[...] = h_new

    yb = h_new * jax.nn.gelu(_in_proj(h, win_v, 3))
    xo_ref[...] = _merge(x, ya.astype(BF16), yb, _in_proj(h, win_v, 4), _in_proj(h, win_v, 5),
                         pa_v, pb_v, wo_v)


_SMALL_PARAMS = ("mix_norm", "v_norm", "spatial_w", "spatial_b_t", "spatial_w0", "spatial_b0",
                 "conv_w", "conv_b", "w_r", "b_r", "w_i", "b_i", "lam")


def _mixer_kernel(*refs, n_tiles, tiles_per_seq):
    n_small = len(_SMALL_PARAMS)
    xp_ref, xs_ref, h0_ref, cprev_ref = refs[:4]
    p = dict(zip(_SMALL_PARAMS, refs[4:4 + n_small]))
    win_hbm, pa_hbm, pb_hbm, wo_hbm = refs[4 + n_small:8 + n_small]
    xo_p, hlast_ref, convnew_ref, xo_s, hnew_ref, xbnew_ref, v_ref = refs[8 + n_small:15 + n_small]
    (win_v, pa_v, pb_v, wo_v, stage_in, stage_sq, sems,
     xb_scr, a_scr, b_scr, ya_scr, h_scr) = refs[15 + n_small:]
    step = pl.program_id(0)

    @pl.when(step == 0)
    def _():
        _stream_weights_to_bf16([(win_hbm, win_v, stage_in), (pa_hbm, pa_v, stage_sq),
                                 (pb_hbm, pb_v, stage_sq), (wo_hbm, wo_v, stage_sq)], sems)

    @pl.when(step < n_tiles)
    def _():
        _mixer_prompt_tile(xp_ref, p, win_v, pa_v, pb_v, wo_v, xo_p, hlast_ref, convnew_ref,
                           xb_scr, a_scr, b_scr, ya_scr, h_scr, step % tiles_per_seq == 0)

    @pl.when(step == n_tiles)
    def _():
        _mixer_sample_rows(xs_ref, h0_ref, cprev_ref, p, win_v, pa_v, pb_v, wo_v,
                           xo_s, hnew_ref, xbnew_ref, v_ref)


def _mixer(xp, xs, h0, cprev, small, win, pa, pb, wo):
    bsz, seq, d = xp.shape
    ms = xs.shape[0]
    tm = MIX_TILE
    tiles_per_seq = seq // tm
    n_tiles = bsz * tiles_per_seq
    win_cols = win.shape[1]

    def tile_of(i):
        j = jnp.minimum(i, n_tiles - 1)
        return j // tiles_per_seq, j % tiles_per_seq

    prompt_rows = pl.BlockSpec((None, tm, d), lambda i: (*tile_of(i), 0))
    per_seq = lambda rows: pl.BlockSpec((None, rows, d), lambda i: (tile_of(i)[0], 0, 0))
    sample_rows = pl.BlockSpec((ms, d), lambda i: (0, 0))
    small_args = [small[k] for k in _SMALL_PARAMS]
    stage_in, stage_sq = _stage_shape(d, win_cols), _stage_shape(d, d)
    vmem = (2 * d * (win_cols + 3 * d)
            + 4 * (stage_in[0] * stage_in[1] * stage_in[2] + stage_sq[0] * stage_sq[1] * stage_sq[2])
            + sum(a.size * a.dtype.itemsize for a in small_args) * 2
            + 4 * 4 * tm * d + 4 * 16 * ms * d
            + (4 * 3 + 2) * tm * d + 4 * HIST * d
            + 4 * 4 * tm * d)
    return pl.pallas_call(
        functools.partial(_mixer_kernel, n_tiles=n_tiles, tiles_per_seq=tiles_per_seq),
        grid=(n_tiles + 1,),
        in_specs=[prompt_rows, sample_rows, sample_rows, _resident(cprev.shape)]
                 + [_resident(a.shape) for a in small_args] + [_HBM] * 4,
        out_specs=[prompt_rows, per_seq(1), per_seq(CONV_WIDTH - 1)] + [sample_rows] * 4,
        out_shape=[jax.ShapeDtypeStruct((bsz, seq, d), F32),
                   jax.ShapeDtypeStruct((bsz, 1, d), F32),
                   jax.ShapeDtypeStruct((bsz, CONV_WIDTH - 1, d), F32)]
                  + [jax.ShapeDtypeStruct((ms, d), F32)] * 4,
        scratch_shapes=[pltpu.VMEM((d, win_cols), BF16), pltpu.VMEM((d, d), BF16),
                        pltpu.VMEM((d, d), BF16), pltpu.VMEM((d, d), BF16),
                        pltpu.VMEM(stage_in, F32), pltpu.VMEM(stage_sq, F32),
                        pltpu.SemaphoreType.DMA((STAGE_SLOTS,)),
                        pltpu.VMEM((HIST + tm, d), F32),
                        pltpu.VMEM((tm, d), F32),
                        pltpu.VMEM((tm, d), F32),
                        pltpu.VMEM((tm, d), BF16),
                        pltpu.VMEM((1, d), F32)],
        compiler_params=pltpu.CompilerParams(dimension_semantics=("arbitrary",), vmem_limit_bytes=vmem),
    )(xp, xs, h0, cprev, *small_args, win, pa, pb, wo)


def _block_diag_tiles(w):
    w = w.reshape(GATE_BLOCKS, HEADS_PER_BLOCK, B_HEAD_DIM, B_HEAD_DIM)
    eye = jnp.eye(HEADS_PER_BLOCK, dtype=w.dtype)
    t = jnp.einsum('bhij,hk->bhikj', w, eye)
    return t.reshape(GATE_BLOCKS, MXU_DIM, MXU_DIM)


def kernel(x_prompt, x_sample, state_lru_h, state_conv, ffn1_norm, ffn1_w_gate, ffn1_w_up, ffn1_w_down, mix_norm, w_in, gmlp_v_norm, spatial_w, spatial_b, conv_w, conv_b, lru_w_r, lru_b_r, lru_w_i, lru_b_i, lru_lambda, proj_a, proj_b, w_out, ffn2_norm, ffn2_w_gate, ffn2_w_up, ffn2_w_down, final_norm):
    depth = ffn1_norm.shape[0]
    bsz, seq, d = x_prompt.shape
    nsamp = x_sample.shape[0]
    row = lambda p: p.reshape(1, -1)
    fin = row(final_norm)

    xp = x_prompt
    xs = x_sample.reshape(nsamp, d)
    h_p, c_p, h_s, c_s, v_s = [], [], [], [], []
    for l in range(depth):
        small = {
            "mix_norm": row(mix_norm[l]), "v_norm": row(gmlp_v_norm[l]),
            "spatial_w": spatial_w[l], "spatial_b_t": spatial_b[l].T,
            "spatial_w0": row(jnp.repeat(spatial_w[l][:, 0, 0], A_GROUP_DIM)),
            "spatial_b0": row(jnp.repeat(spatial_b[l][:, 0], A_GROUP_DIM)),
            "conv_w": conv_w[l], "conv_b": row(conv_b[l]),
            "w_r": _block_diag_tiles(lru_w_r[l]).astype(BF16), "b_r": row(lru_b_r[l]),
            "w_i": _block_diag_tiles(lru_w_i[l]).astype(BF16), "b_i": row(lru_b_i[l]),
            "lam": row(lru_lambda[l]),
        }
        last = l == depth - 1

        xp2d, xs = _ffn(xp.reshape(bsz * seq, d), xs, row(ffn1_norm[l]), fin,
                        ffn1_w_gate[l].T, ffn1_w_up[l].T, ffn1_w_down[l], final_norm=False)
        cprev = jnp.swapaxes(state_conv[l], 0, 1)
        xp, hl, cn, xs, hn, xb_new, v_new = _mixer(
            xp2d.reshape(bsz, seq, d), xs, state_lru_h[l], cprev, small,
            w_in[l], proj_a[l], proj_b[l], w_out[l])
        xp2d, xs = _ffn(xp.reshape(bsz * seq, d), xs, row(ffn2_norm[l]), fin,
                        ffn2_w_gate[l].T, ffn2_w_up[l].T, ffn2_w_down[l], final_norm=last)
        xp = xp2d.reshape(bsz, seq, d)

        h_p.append(hl.reshape(bsz, d))
        c_p.append(cn)
        h_s.append(hn)
        c_s.append(jnp.concatenate([state_conv[l][:, 1:], xb_new[:, None, :]], axis=1))
        v_s.append(v_new.reshape(nsamp, 1, d))

    return (xp, xs.reshape(nsamp, 1, d), jnp.stack(h_p), jnp.stack(c_p),
            jnp.stack(h_s), jnp.stack(c_s), jnp.stack(v_s))
```

```python
import functools

import jax
import jax.numpy as jnp
from jax import lax
from jax.experimental import pallas as pl
from jax.experimental.pallas import tpu as pltpu

D_MODEL = 1024
CHUNK = 128
A_GROUPS = 8
A_GROUP_DIM = D_MODEL // A_GROUPS
B_HEADS = 16
B_HEAD_DIM = D_MODEL // B_HEADS
CONV_WIDTH = 4
LRU_C = 8.0
EPS = 1e-6

SUBLANES = 8
BF16_ROWS = 16
MXU_DIM = 256
GATE_BLOCKS = D_MODEL // MXU_DIM
HEADS_PER_BLOCK = MXU_DIM // B_HEAD_DIM

FFN_TILE = 512
MIX_TILE = 256
HIST = SUBLANES
STAGE_SLOTS = 3
STAGE_BYTES = 3 << 20

F32 = jnp.float32
BF16 = jnp.bfloat16


def _rms(x, g):
    return x * lax.rsqrt(jnp.mean(x * x, axis=-1, keepdims=True) + EPS) * g


def _sigmoid(x):
    return 0.5 * jnp.tanh(0.5 * x) + 0.5


def _resident(shape):
    nd = len(shape)
    return pl.BlockSpec(shape, lambda *_: (0,) * nd, pipeline_mode=pl.Buffered(1))


_HBM = pl.BlockSpec(memory_space=pl.ANY)


def _chunk_rows(k, n):
    best = BF16_ROWS
    for r in range(BF16_ROWS, k + 1, BF16_ROWS):
        if k % r == 0 and r * n * 4 <= STAGE_BYTES:
            best = r
    return best


def _stage_shape(k, n):
    return (STAGE_SLOTS, _chunk_rows(k, n), n)


def _stream_weights_to_bf16(jobs, sems):
    chunks = []
    for src, dst, stage in jobs:
        rows = stage.shape[1]
        for c in range(src.shape[0] // rows):
            chunks.append((src, dst, stage, c * rows, rows))

    def copy(j):
        src, _, stage, r0, rows = chunks[j]
        slot = j % STAGE_SLOTS
        return pltpu.make_async_copy(src.at[pl.ds(r0, rows), :], stage.at[slot], sems.at[slot])

    ahead = STAGE_SLOTS - 1
    for j in range(min(ahead, len(chunks))):
        copy(j).start()
    for j, (_, dst, stage, r0, rows) in enumerate(chunks):
        if j + ahead < len(chunks):
            copy(j + ahead).start()
        copy(j).wait()
        dst[r0:r0 + rows, :] = stage[j % STAGE_SLOTS].astype(BF16)


def _dot_nt(a, b_t):
    return lax.dot_general(a, b_t, (((1,), (1,)), ((), ())), preferred_element_type=F32)


def _ffn_rows(x, g_ref, wg_v, wu_v, wd_v, fg_ref, final_norm):
    h = _rms(x, g_ref[...]).astype(BF16)
    gate = _dot_nt(h, wg_v[...])
    up = _dot_nt(h, wu_v[...])
    act = (gate * _sigmoid(gate) * up).astype(BF16)
    y = x + 0.5 * jnp.dot(act, wd_v[...], preferred_element_type=F32)
    return _rms(y, fg_ref[...]) if final_norm else y


def _ffn_kernel(xp_ref, xs_ref, g_ref, fg_ref, wg_hbm, wu_hbm, wd_hbm,
                op_ref, os_ref,
                wg_v, wu_v, wd_v, stage, sems, *, n_tiles, final_norm):
    step = pl.program_id(0)

    @pl.when(step == 0)
    def _():
        _stream_weights_to_bf16([(wg_hbm, wg_v, stage), (wu_hbm, wu_v, stage), (wd_hbm, wd_v, stage)], sems)

    @pl.when(step < n_tiles)
    def _():
        op_ref[...] = _ffn_rows(xp_ref[...], g_ref, wg_v, wu_v, wd_v, fg_ref, final_norm)

    @pl.when(step == n_tiles)
    def _():
        os_ref[...] = _ffn_rows(xs_ref[...], g_ref, wg_v, wu_v, wd_v, fg_ref, final_norm)


def _ffn(xp2d, xs2d, g, fg, wg_t, wu_t, wd, *, final_norm):
    m, d = xp2d.shape
    ms = xs2d.shape[0]
    dff = wd.shape[0]
    tm = FFN_TILE
    n_tiles = m // tm
    prompt_rows = pl.BlockSpec((tm, d), lambda i: (jnp.minimum(i, n_tiles - 1), 0))
    sample_rows = pl.BlockSpec((ms, d), lambda i: (0, 0))
    stage = _stage_shape(dff, d)
    vmem = (2 * 3 * d * dff
            + 4 * stage[0] * stage[1] * stage[2]
            + 4 * 4 * (tm + ms) * d
            + tm * (2 * 4 + 2 + 2) * dff + 4 * 4 * tm * d)
    return pl.pallas_call(
        functools.partial(_ffn_kernel, n_tiles=n_tiles, final_norm=final_norm),
        grid=(n_tiles + 1,),
        in_specs=[prompt_rows, sample_rows, _resident(g.shape), _resident(fg.shape), _HBM, _HBM, _HBM],
        out_specs=[prompt_rows, sample_rows],
        out_shape=[jax.ShapeDtypeStruct((m, d), F32), jax.ShapeDtypeStruct((ms, d), F32)],
        scratch_shapes=[pltpu.VMEM((dff, d), BF16)] * 3
                       + [pltpu.VMEM(stage, F32), pltpu.SemaphoreType.DMA((STAGE_SLOTS,))],
        compiler_params=pltpu.CompilerParams(dimension_semantics=("arbitrary",), vmem_limit_bytes=vmem),
    )(xp2d, xs2d, g, fg, wg_t, wu_t, wd)


def _in_proj(h, win_v, k):
    return jnp.dot(h, win_v[:, k * D_MODEL:(k + 1) * D_MODEL], preferred_element_type=F32)


def _lru_coeffs(xc, wr_ref, br_ref, wi_ref, bi_ref, lam_ref):
    xcb = xc.astype(BF16)
    r_parts, i_parts = [], []
    for j in range(GATE_BLOCKS):
        blk = xcb[:, j * MXU_DIM:(j + 1) * MXU_DIM]
        r_parts.append(jnp.dot(blk, wr_ref[j], preferred_element_type=F32))
        i_parts.append(jnp.dot(blk, wi_ref[j], preferred_element_type=F32))
    lam = lam_ref[...]
    half_rate = (-0.5 * LRU_C) * (jnp.maximum(-lam, 0.0) + jnp.log1p(jnp.exp(-jnp.abs(lam))))
    t_r = jnp.tanh(0.5 * jnp.concatenate(r_parts, axis=-1) + 0.5 * br_ref[...])
    t_i = jnp.tanh(0.5 * jnp.concatenate(i_parts, axis=-1) + 0.5 * bi_ref[...])
    log_a = t_r * half_rate + half_rate
    a = jnp.exp(log_a)
    th = jnp.tanh(log_a)
    num = -2.0 * th
    root = jnp.where(num == 0.0, 0.0, num * lax.rsqrt(num)) * lax.rsqrt(1.0 - th)
    half_xc = 0.5 * xc
    b = root * (t_i * half_xc + half_xc)
    return a, b


def _merge(x, ya_b16, yb, g_a, g_b, pa_v, pb_v, wo_v):
    m = (_sigmoid(g_a) * jnp.dot(ya_b16, pa_v[...], preferred_element_type=F32)
         + _sigmoid(g_b) * jnp.dot(yb.astype(BF16), pb_v[...], preferred_element_type=F32))
    return x + jnp.dot(m.astype(BF16), wo_v[...], preferred_element_type=F32)


def _mixer_prompt_tile(x_ref, p, win_v, pa_v, pb_v, wo_v, xo_ref, hlast_ref, convnew_ref,
                       xb_scr, a_scr, b_scr, ya_scr, h_scr, first_tile):
    tm = x_ref.shape[0]

    @pl.when(first_tile)
    def _():
        xb_scr[0:HIST, :] = jnp.zeros((HIST, D_MODEL), F32)
        h_scr[...] = jnp.zeros((1, D_MODEL), F32)

    x = x_ref[...]
    h = _rms(x, p["mix_norm"][...]).astype(BF16)

    u = jax.nn.gelu(_in_proj(h, win_v, 0))
    v = _rms(jax.nn.gelu(_in_proj(h, win_v, 1)), p["v_norm"][...]).astype(BF16)
    causal = (lax.broadcasted_iota(jnp.int32, (CHUNK, CHUNK), 0)
              >= lax.broadcasted_iota(jnp.int32, (CHUNK, CHUNK), 1))
    for g in range(A_GROUPS):
        w_g = jnp.where(causal, p["spatial_w"][g], 0.0).astype(BF16)
        bias_g = p["spatial_b_t"][:, g:g + 1]
        cols = slice(g * A_GROUP_DIM, (g + 1) * A_GROUP_DIM)
        for c in range(tm // CHUNK):
            rows = slice(c * CHUNK, (c + 1) * CHUNK)
            s = jnp.dot(w_g, v[rows, cols], preferred_element_type=F32) + bias_g
            ya_scr[rows, cols] = (u[rows, cols] * s).astype(BF16)

    xb = _in_proj(h, win_v, 2)
    xb_scr[HIST:HIST + tm, :] = xb
    cw = p["conv_w"]
    xc = p["conv_b"][...] + cw[CONV_WIDTH - 1:CONV_WIDTH, :] * xb
    for k in range(CONV_WIDTH - 1):
        back = CONV_WIDTH - 1 - k
        xc = xc + cw[k:k + 1, :] * xb_scr[HIST - back:HIST - back + tm, :]
    convnew_ref[...] = xb_scr[HIST + tm - (CONV_WIDTH - 1):HIST + tm, :]
    xb_scr[0:HIST, :] = xb_scr[tm:tm + HIST, :]

    a, b = _lru_coeffs(xc, p["w_r"], p["b_r"], p["w_i"], p["b_i"], p["lam"])
    a_scr[...] = a
    b_scr[...] = b

    first_row = lax.broadcasted_iota(jnp.int32, (SUBLANES, D_MODEL), 0) == 0
    carry = h_scr[...]
    for r0 in range(0, tm, SUBLANES):
        ag = a_scr[r0:r0 + SUBLANES, :]
        hg = jnp.where(first_row, ag, 0.0) * carry + b_scr[r0:r0 + SUBLANES, :]
        span = jnp.where(first_row, 0.0, ag)
        d = 1
        while d < SUBLANES:
            hg = span * pltpu.roll(hg, d, 0) + hg
            if 2 * d < SUBLANES:
                span = span * pltpu.roll(span, d, 0)
            d *= 2
        b_scr[r0:r0 + SUBLANES, :] = hg
        carry = hg[SUBLANES - 1:SUBLANES, :]
    h_scr[...] = carry
    hlast_ref[...] = carry

    yb = b_scr[...] * jax.nn.gelu(_in_proj(h, win_v, 3))
    xo_ref[...] = _merge(x, ya_scr[...], yb, _in_proj(h, win_v, 4), _in_proj(h, win_v, 5),
                         pa_v, pb_v, wo_v)


def _mixer_sample_rows(x_ref, h0_ref, cprev_ref, p, win_v, pa_v, pb_v, wo_v,
                       xo_ref, hnew_ref, xbnew_ref, v_ref):
    x = x_ref[...]
    h = _rms(x, p["mix_norm"][...]).astype(BF16)

    u = jax.nn.gelu(_in_proj(h, win_v, 0))
    v = _rms(jax.nn.gelu(_in_proj(h, win_v, 1)), p["v_norm"][...])
    v_ref[...] = v
    ya = u * (p["spatial_w0"][...] * v + p["spatial_b0"][...])

    xb = _in_proj(h, win_v, 2)
    xbnew_ref[...] = xb
    cw = p["conv_w"]
    xc = p["conv_b"][...] + cw[CONV_WIDTH - 1:CONV_WIDTH, :] * xb
    for k in range(CONV_WIDTH - 1):
        xc = xc + cw[k:k + 1, :] * cprev_ref[k]

    a, b = _lru_coeffs(xc, p["w_r"], p["b_r"], p["w_i"], p["b_i"], p["lam"])
    h_new = a * h0_ref[...] + b
    hnew_ref[...] = h_new

    yb = h_new * jax.nn.gelu(_in_proj(h, win_v, 3))
    xo_ref[...] = _merge(x, ya.astype(BF16), yb, _in_proj(h, win_v, 4), _in_proj(h, win_v, 5),
                         pa_v, pb_v, wo_v)


_SMALL_PARAMS = ("mix_norm", "v_norm", "spatial_w", "spatial_b_t", "spatial_w0", "spatial_b0",
                 "conv_w", "conv_b", "w_r", "b_r", "w_i", "b_i", "lam")


def _mixer_kernel(*refs, n_tiles, tiles_per_seq):
    n_small = len(_SMALL_PARAMS)
    xp_ref, xs_ref, h0_ref, cprev_ref = refs[:4]
    p = dict(zip(_SMALL_PARAMS, refs[4:4 + n_small]))
    win_hbm, pa_hbm, pb_hbm, wo_hbm = refs[4 + n_small:8 + n_small]
    xo_p, hlast_ref, convnew_ref, xo_s, hnew_ref, xbnew_ref, v_ref = refs[8 + n_small:15 + n_small]
    (win_v, pa_v, pb_v, wo_v, stage_in, stage_sq, sems,
     xb_scr, a_scr, b_scr, ya_scr, h_scr) = refs[15 + n_small:]
    step = pl.program_id(0)

    @pl.when(step == 0)
    def _():
        _stream_weights_to_bf16([(win_hbm, win_v, stage_in), (pa_hbm, pa_v, stage_sq),
                                 (pb_hbm, pb_v, stage_sq), (wo_hbm, wo_v, stage_sq)], sems)

    @pl.when(step < n_tiles)
    def _():
        _mixer_prompt_tile(xp_ref, p, win_v, pa_v, pb_v, wo_v, xo_p, hlast_ref, convnew_ref,
                           xb_scr, a_scr, b_scr, ya_scr, h_scr, step % tiles_per_seq == 0)

    @pl.when(step == n_tiles)
    def _():
        _mixer_sample_rows(xs_ref, h0_ref, cprev_ref, p, win_v, pa_v, pb_v, wo_v,
                           xo_s, hnew_ref, xbnew_ref, v_ref)


def _mixer(xp, xs, h0, cprev, small, win, pa, pb, wo):
    bsz, seq, d = xp.shape
    ms = xs.shape[0]
    tm = MIX_TILE
    tiles_per_seq = seq // tm
    n_tiles = bsz * tiles_per_seq
    win_cols = win.shape[1]

    def tile_of(i):
        j = jnp.minimum(i, n_tiles - 1)
        return j // tiles_per_seq, j % tiles_per_seq

    prompt_rows = pl.BlockSpec((None, tm, d), lambda i: (*tile_of(i), 0))
    per_seq = lambda rows: pl.BlockSpec((None, rows, d), lambda i: (tile_of(i)[0], 0, 0))
    sample_rows = pl.BlockSpec((ms, d), lambda i: (0, 0))
    small_args = [small[k] for k in _SMALL_PARAMS]
    stage_in, stage_sq = _stage_shape(d, win_cols), _stage_shape(d, d)
    vmem = (2 * d * (win_cols + 3 * d)
            + 4 * (stage_in[0] * stage_in[1] * stage_in[2] + stage_sq[0] * stage_sq[1] * stage_sq[2])
            + sum(a.size * a.dtype.itemsize for a in small_args) * 2
            + 4 * 4 * tm * d + 4 * 16 * ms * d
            + (4 * 3 + 2) * tm * d + 4 * HIST * d
            + 4 * 4 * tm * d)
    return pl.pallas_call(
        functools.partial(_mixer_kernel, n_tiles=n_tiles, tiles_per_seq=tiles_per_seq),
        grid=(n_tiles + 1,),
        in_specs=[prompt_rows, sample_rows, sample_rows, _resident(cprev.shape)]
                 + [_resident(a.shape) for a in small_args] + [_HBM] * 4,
        out_specs=[prompt_rows, per_seq(1), per_seq(CONV_WIDTH - 1)] + [sample_rows] * 4,
        out_shape=[jax.ShapeDtypeStruct((bsz, seq, d), F32),
                   jax.ShapeDtypeStruct((bsz, 1, d), F32),
                   jax.ShapeDtypeStruct((bsz, CONV_WIDTH - 1, d), F32)]
                  + [jax.ShapeDtypeStruct((ms, d), F32)] * 4,
        scratch_shapes=[pltpu.VMEM((d, win_cols), BF16), pltpu.VMEM((d, d), BF16),
                        pltpu.VMEM((d, d), BF16), pltpu.VMEM((d, d), BF16),
                        pltpu.VMEM(stage_in, F32), pltpu.VMEM(stage_sq, F32),
                        pltpu.SemaphoreType.DMA((STAGE_SLOTS,)),
                        pltpu.VMEM((HIST + tm, d), F32),
                        pltpu.VMEM((tm, d), F32),
                        pltpu.VMEM((tm, d), F32),
                        pltpu.VMEM((tm, d), BF16),
                        pltpu.VMEM((1, d), F32)],
        compiler_params=pltpu.CompilerParams(dimension_semantics=("arbitrary",), vmem_limit_bytes=vmem),
    )(xp, xs, h0, cprev, *small_args, win, pa, pb, wo)


def _block_diag_tiles(w):
    w = w.reshape(GATE_BLOCKS, HEADS_PER_BLOCK, B_HEAD_DIM, B_HEAD_DIM)
    eye = jnp.eye(HEADS_PER_BLOCK, dtype=w.dtype)
    t = jnp.einsum('bhij,hk->bhikj', w, eye)
    return t.reshape(GATE_BLOCKS, MXU_DIM, MXU_DIM)


def kernel(x_prompt, x_sample, state_lru_h, state_conv, ffn1_norm, ffn1_w_gate, ffn1_w_up, ffn1_w_down, mix_norm, w_in, gmlp_v_norm, spatial_w, spatial_b, conv_w, conv_b, lru_w_r, lru_b_r, lru_w_i, lru_b_i, lru_lambda, proj_a, proj_b, w_out, ffn2_norm, ffn2_w_gate, ffn2_w_up, ffn2_w_down, final_norm):
    depth = ffn1_norm.shape[0]
    bsz, seq, d = x_prompt.shape
    nsamp = x_sample.shape[0]
    row = lambda p: p.reshape(1, -1)
    fin = row(final_norm)

    xp = x_prompt
    xs = x_sample.reshape(nsamp, d)
    h_p, c_p, h_s, c_s, v_s = [], [], [], [], []
    for l in range(depth):
        small = {
            "mix_norm": row(mix_norm[l]), "v_norm": row(gmlp_v_norm[l]),
            "spatial_w": spatial_w[l], "spatial_b_t": spatial_b[l].T,
            "spatial_w0": row(jnp.repeat(spatial_w[l][:, 0, 0], A_GROUP_DIM)),
            "spatial_b0": row(jnp.repeat(spatial_b[l][:, 0], A_GROUP_DIM)),
            "conv_w": conv_w[l], "conv_b": row(conv_b[l]),
            "w_r": _block_diag_tiles(lru_w_r[l]).astype(BF16), "b_r": row(lru_b_r[l]),
            "w_i": _block_diag_tiles(lru_w_i[l]).astype(BF16), "b_i": row(lru_b_i[l]),
            "lam": row(lru_lambda[l]),
        }
        last = l == depth - 1

        xp2d, xs = _ffn(xp.reshape(bsz * seq, d), xs, row(ffn1_norm[l]), fin,
                        ffn1_w_gate[l].T, ffn1_w_up[l].T, ffn1_w_down[l], final_norm=False)
        cprev = jnp.swapaxes(state_conv[l], 0, 1)
        xp, hl, cn, xs, hn, xb_new, v_new = _mixer(
            xp2d.reshape(bsz, seq, d), xs, state_lru_h[l], cprev, small,
            w_in[l], proj_a[l], proj_b[l], w_out[l])
        xp2d, xs = _ffn(xp.reshape(bsz * seq, d), xs, row(ffn2_norm[l]), fin,
                        ffn2_w_gate[l].T, ffn2_w_up[l].T, ffn2_w_down[l], final_norm=last)
        xp = xp2d.reshape(bsz, seq, d)

        h_p.append(hl.reshape(bsz, d))
        c_p.append(cn)
        h_s.append(hn)
        c_s.append(jnp.concatenate([state_conv[l][:, 1:], xb_new[:, None, :]], axis=1))
        v_s.append(v_new.reshape(nsamp, 1, d))

    return (xp, xs.reshape(nsamp, 1, d), jnp.stack(h_p), jnp.stack(c_p),
            jnp.stack(h_s), jnp.stack(c_s), jnp.stack(v_s))
```

```python
import functools

import jax
import jax.numpy as jnp
from jax import lax
from jax.experimental import pallas as pl
from jax.experimental.pallas import tpu as pltpu

D_MODEL = 1024
CHUNK = 128
A_GROUPS = 8
A_GROUP_DIM = D_MODEL // A_GROUPS
B_HEADS = 16
B_HEAD_DIM = D_MODEL // B_HEADS
CONV_WIDTH = 4
LRU_C = 8.0
EPS = 1e-6

SUBLANES = 8
BF16_ROWS = 16
MXU_DIM = 256
GATE_BLOCKS = D_MODEL // MXU_DIM
HEADS_PER_BLOCK = MXU_DIM // B_HEAD_DIM

FFN_TILE = 512
MIX_TILE = 256
HIST = SUBLANES
STAGE_SLOTS = 4
STAGE_BYTES = 3 << 20

F32 = jnp.float32
BF16 = jnp.bfloat16


def _rms(x, g):
    return x * lax.rsqrt(jnp.mean(x * x, axis=-1, keepdims=True) + EPS) * g


def _sigmoid(x):
    return 0.5 * jnp.tanh(0.5 * x) + 0.5


def _resident(shape):
    nd = len(shape)
    return pl.BlockSpec(shape, lambda *_: (0,) * nd, pipeline_mode=pl.Buffered(1))


_HBM = pl.BlockSpec(memory_space=pl.ANY)


def _chunk_rows(k, n):
    best = BF16_ROWS
    for r in range(BF16_ROWS, k + 1, BF16_ROWS):
        if k % r == 0 and r * n * 4 <= STAGE_BYTES:
            best = r
    return best


def _stage_shape(k, n):
    return (STAGE_SLOTS, _chunk_rows(k, n), n)


def _stream_weights_to_bf16(jobs, sems):
    chunks = []
    for src, dst, stage in jobs:
        rows = stage.shape[1]
        for c in range(src.shape[0] // rows):
            chunks.append((src, dst, stage, c * rows, rows))

    def copy(j):
        src, _, stage, r0, rows = chunks[j]
        slot = j % STAGE_SLOTS
        return pltpu.make_async_copy(src.at[pl.ds(r0, rows), :], stage.at[slot], sems.at[slot])

    ahead = STAGE_SLOTS - 1
    for j in range(min(ahead, len(chunks))):
        copy(j).start()
    for j, (_, dst, stage, r0, rows) in enumerate(chunks):
        if j + ahead < len(chunks):
            copy(j + ahead).start()
        copy(j).wait()
        dst[r0:r0 + rows, :] = stage[j % STAGE_SLOTS].astype(BF16)


def _dot_nt(a, b_t):
    return lax.dot_general(a, b_t, (((1,), (1,)), ((), ())), preferred_element_type=F32)


def _ffn_rows(x, g_ref, wg_v, wu_v, wd_v, fg_ref, final_norm):
    h = _rms(x, g_ref[...]).astype(BF16)
    gate = _dot_nt(h, wg_v[...])
    up = _dot_nt(h, wu_v[...])
    act = (gate * _sigmoid(gate) * up).astype(BF16)
    y = x + 0.5 * jnp.dot(act, wd_v[...], preferred_element_type=F32)
    return _rms(y, fg_ref[...]) if final_norm else y


def _ffn_kernel(xp_ref, xs_ref, g_ref, fg_ref, wg_hbm, wu_hbm, wd_hbm,
                op_ref, os_ref,
                wg_v, wu_v, wd_v, stage, sems, *, n_tiles, final_norm):
    step = pl.program_id(0)

    @pl.when(step == 0)
    def _():
        _stream_weights_to_bf16([(wg_hbm, wg_v, stage), (wu_hbm, wu_v, stage), (wd_hbm, wd_v, stage)], sems)

    @pl.when(step < n_tiles)
    def _():
        op_ref[...] = _ffn_rows(xp_ref[...], g_ref, wg_v, wu_v, wd_v, fg_ref, final_norm)

    @pl.when(step == n_tiles)
    def _():
        os_ref[...] = _ffn_rows(xs_ref[...], g_ref, wg_v, wu_v, wd_v, fg_ref, final_norm)


def _ffn(xp2d, xs2d, g, fg, wg_t, wu_t, wd, *, final_norm):
    m, d = xp2d.shape
    ms = xs2d.shape[0]
    dff = wd.shape[0]
    tm = FFN_TILE
    n_tiles = m // tm
    prompt_rows = pl.BlockSpec((tm, d), lambda i: (jnp.minimum(i, n_tiles - 1), 0))
    sample_rows = pl.BlockSpec((ms, d), lambda i: (0, 0))
    stage = _stage_shape(dff, d)
    vmem = (2 * 3 * d * dff
            + 4 * stage[0] * stage[1] * stage[2]
            + 4 * 4 * (tm + ms) * d
            + tm * (2 * 4 + 2 + 2) * dff + 4 * 4 * tm * d)
    return pl.pallas_call(
        functools.partial(_ffn_kernel, n_tiles=n_tiles, final_norm=final_norm),
        grid=(n_tiles + 1,),
        in_specs=[prompt_rows, sample_rows, _resident(g.shape), _resident(fg.shape), _HBM, _HBM, _HBM],
        out_specs=[prompt_rows, sample_rows],
        out_shape=[jax.ShapeDtypeStruct((m, d), F32), jax.ShapeDtypeStruct((ms, d), F32)],
        scratch_shapes=[pltpu.VMEM((dff, d), BF16)] * 3
                       + [pltpu.VMEM(stage, F32), pltpu.SemaphoreType.DMA((STAGE_SLOTS,))],
        compiler_params=pltpu.CompilerParams(dimension_semantics=("arbitrary",), vmem_limit_bytes=vmem),
    )(xp2d, xs2d, g, fg, wg_t, wu_t, wd)


def _in_proj(h, win_v, k):
    return jnp.dot(h, win_v[:, k * D_MODEL:(k + 1) * D_MODEL], preferred_element_type=F32)


def _lru_coeffs(xc, wr_ref, br_ref, wi_ref, bi_ref, lam_ref):
    xcb = xc.astype(BF16)
    r_parts, i_parts = [], []
    for j in range(GATE_BLOCKS):
        blk = xcb[:, j * MXU_DIM:(j + 1) * MXU_DIM]
        r_parts.append(jnp.dot(blk, wr_ref[j], preferred_element_type=F32))
        i_parts.append(jnp.dot(blk, wi_ref[j], preferred_element_type=F32))
    lam = lam_ref[...]
    half_rate = (-0.5 * LRU_C) * (jnp.maximum(-lam, 0.0) + jnp.log1p(jnp.exp(-jnp.abs(lam))))
    t_r = jnp.tanh(0.5 * jnp.concatenate(r_parts, axis=-1) + 0.5 * br_ref[...])
    t_i = jnp.tanh(0.5 * jnp.concatenate(i_parts, axis=-1) + 0.5 * bi_ref[...])
    log_a = t_r * half_rate + half_rate
    a = jnp.exp(log_a)
    th = jnp.tanh(log_a)
    num = -2.0 * th
    root = jnp.where(num == 0.0, 0.0, num * lax.rsqrt(num)) * lax.rsqrt(1.0 - th)
    half_xc = 0.5 * xc
    b = root * (t_i * half_xc + half_xc)
    return a, b


def _merge(x, ya_b16, yb, g_a, g_b, pa_v, pb_v, wo_v):
    m = (_sigmoid(g_a) * jnp.dot(ya_b16, pa_v[...], preferred_element_type=F32)
         + _sigmoid(g_b) * jnp.dot(yb.astype(BF16), pb_v[...], preferred_element_type=F32))
    return x + jnp.dot(m.astype(BF16), wo_v[...], preferred_element_type=F32)


def _mixer_prompt_tile(x_ref, p, win_v, pa_v, pb_v, wo_v, xo_ref, hlast_ref, convnew_ref,
                       xb_scr, a_scr, b_scr, ya_scr, h_scr, first_tile):
    tm = x_ref.shape[0]

    @pl.when(first_tile)
    def _():
        xb_scr[0:HIST, :] = jnp.zeros((HIST, D_MODEL), F32)
        h_scr[...] = jnp.zeros((1, D_MODEL), F32)

    x = x_ref[...]
    h = _rms(x, p["mix_norm"][...]).astype(BF16)

    u = jax.nn.gelu(_in_proj(h, win_v, 0))
    v = _rms(jax.nn.gelu(_in_proj(h, win_v, 1)), p["v_norm"][...]).astype(BF16)
    causal = (lax.broadcasted_iota(jnp.int32, (CHUNK, CHUNK), 0)
              >= lax.broadcasted_iota(jnp.int32, (CHUNK, CHUNK), 1))
    for g in range(A_GROUPS):
        w_g = jnp.where(causal, p["spatial_w"][g], 0.0).astype(BF16)
        bias_g = p["spatial_b_t"][:, g:g + 1]
        cols = slice(g * A_GROUP_DIM, (g + 1) * A_GROUP_DIM)
        for c in range(tm // CHUNK):
            rows = slice(c * CHUNK, (c + 1) * CHUNK)
            s = jnp.dot(w_g, v[rows, cols], preferred_element_type=F32) + bias_g
            ya_scr[rows, cols] = (u[rows, cols] * s).astype(BF16)

    xb = _in_proj(h, win_v, 2)
    xb_scr[HIST:HIST + tm, :] = xb
    cw = p["conv_w"]
    xc = p["conv_b"][...] + cw[CONV_WIDTH - 1:CONV_WIDTH, :] * xb
    for k in range(CONV_WIDTH - 1):
        back = CONV_WIDTH - 1 - k
        xc = xc + cw[k:k + 1, :] * xb_scr[HIST - back:HIST - back + tm, :]
    convnew_ref[...] = xb_scr[HIST + tm - (CONV_WIDTH - 1):HIST + tm, :]
    xb_scr[0:HIST, :] = xb_scr[tm:tm + HIST, :]

    a, b = _lru_coeffs(xc, p["w_r"], p["b_r"], p["w_i"], p["b_i"], p["lam"])
    a_scr[...] = a
    b_scr[...] = b

    first_row = lax.broadcasted_iota(jnp.int32, (SUBLANES, D_MODEL), 0) == 0
    carry = h_scr[...]
    for r0 in range(0, tm, SUBLANES):
        ag = a_scr[r0:r0 + SUBLANES, :]
        hg = jnp.where(first_row, ag, 0.0) * carry + b_scr[r0:r0 + SUBLANES, :]
        span = jnp.where(first_row, 0.0, ag)
        d = 1
        while d < SUBLANES:
            hg = span * pltpu.roll(hg, d, 0) + hg
            if 2 * d < SUBLANES:
                span = span * pltpu.roll(span, d, 0)
            d *= 2
        b_scr[r0:r0 + SUBLANES, :] = hg
        carry = hg[SUBLANES - 1:SUBLANES, :]
    h_scr[...] = carry
    hlast_ref[...] = carry

    yb = b_scr[...] * jax.nn.gelu(_in_proj(h, win_v, 3))
    xo_ref[...] = _merge(x, ya_scr[...], yb, _in_proj(h, win_v, 4), _in_proj(h, win_v, 5),
                         pa_v, pb_v, wo_v)


def _mixer_sample_rows(x_ref, h0_ref, cprev_ref, p, win_v, pa_v, pb_v, wo_v,
                       xo_ref, hnew_ref, xbnew_ref, v_ref):
    x = x_ref[...]
    h = _rms(x, p["mix_norm"][...]).astype(BF16)

    u = jax.nn.gelu(_in_proj(h, win_v, 0))
    v = _rms(jax.nn.gelu(_in_proj(h, win_v, 1)), p["v_norm"][...])
    v_ref[...] = v
    ya = u * (p["spatial_w0"][...] * v + p["spatial_b0"][...])

    xb = _in_proj(h, win_v, 2)
    xbnew_ref[...] = xb
    cw = p["conv_w"]
    xc = p["conv_b"][...] + cw[CONV_WIDTH - 1:CONV_WIDTH, :] * xb
    for k in range(CONV_WIDTH - 1):
        xc = xc + cw[k:k + 1, :] * cprev_ref[k]

    a, b = _lru_coeffs(xc, p["w_r"], p["b_r"], p["w_i"], p["b_i"], p["lam"])
    h_new = a * h0_ref[...] + b
    hnew_ref[...] = h_new

    yb = h_new * jax.nn.gelu(_in_proj(h, win_v, 3))
    xo_ref[...] = _merge(x, ya.astype(BF16), yb, _in_proj(h, win_v, 4), _in_proj(h, win_v, 5),
                         pa_v, pb_v, wo_v)


_SMALL_PARAMS = ("mix_norm", "v_norm", "spatial_w", "spatial_b_t", "spatial_w0", "spatial_b0",
                 "conv_w", "conv_b", "w_r", "b_r", "w_i", "b_i", "lam")


def _mixer_kernel(*refs, n_tiles, tiles_per_seq):
    n_small = len(_SMALL_PARAMS)
    xp_ref, xs_ref, h0_ref, cprev_ref = refs[:4]
    p = dict(zip(_SMALL_PARAMS, refs[4:4 + n_small]))
    win_hbm, pa_hbm, pb_hbm, wo_hbm = refs[4 + n_small:8 + n_small]
    xo_p, hlast_ref, convnew_ref, xo_s, hnew_ref, xbnew_ref, v_ref = refs[8 + n_small:15 + n_small]
    (win_v, pa_v, pb_v, wo_v, stage_in, stage_sq, sems,
     xb_scr, a_scr, b_scr, ya_scr, h_scr) = refs[15 + n_small:]
    step = pl.program_id(0)

    @pl.when(step == 0)
    def _():
        _stream_weights_to_bf16([(win_hbm, win_v, stage_in), (pa_hbm, pa_v, stage_sq),
                                 (pb_hbm, pb_v, stage_sq), (wo_hbm, wo_v, stage_sq)], sems)

    @pl.when(step < n_tiles)
    def _():
        _mixer_prompt_tile(xp_ref, p, win_v, pa_v, pb_v, wo_v, xo_p, hlast_ref, convnew_ref,
                           xb_scr, a_scr, b_scr, ya_scr, h_scr, step % tiles_per_seq == 0)

    @pl.when(step == n_tiles)
    def _():
        _mixer_sample_rows(xs_ref, h0_ref, cprev_ref, p, win_v, pa_v, pb_v, wo_v,
                           xo_s, hnew_ref, xbnew_ref, v_ref)


def _mixer(xp, xs, h0, cprev, small, win, pa, pb, wo):
    bsz, seq, d = xp.shape
    ms = xs.shape[0]
    tm = MIX_TILE
    tiles_per_seq = seq // tm
    n_tiles = bsz * tiles_per_seq
    win_cols = win.shape[1]

    def tile_of(i):
        j = jnp.minimum(i, n_tiles - 1)
        return j // tiles_per_seq, j % tiles_per_seq

    prompt_rows = pl.BlockSpec((None, tm, d), lambda i: (*tile_of(i), 0))
    per_seq = lambda rows: pl.BlockSpec((None, rows, d), lambda i: (tile_of(i)[0], 0, 0))
    sample_rows = pl.BlockSpec((ms, d), lambda i: (0, 0))
    small_args = [small[k] for k in _SMALL_PARAMS]
    stage_in, stage_sq = _stage_shape(d, win_cols), _stage_shape(d, d)
    vmem = (2 * d * (win_cols + 3 * d)
            + 4 * (stage_in[0] * stage_in[1] * stage_in[2] + stage_sq[0] * stage_sq[1] * stage_sq[2])
            + sum(a.size * a.dtype.itemsize for a in small_args) * 2
            + 4 * 4 * tm * d + 4 * 16 * ms * d
            + (4 * 3 + 2) * tm * d + 4 * HIST * d
            + 4 * 4 * tm * d)
    return pl.pallas_call(
        functools.partial(_mixer_kernel, n_tiles=n_tiles, tiles_per_seq=tiles_per_seq),
        grid=(n_tiles + 1,),
        in_specs=[prompt_rows, sample_rows, sample_rows, _resident(cprev.shape)]
                 + [_resident(a.shape) for a in small_args] + [_HBM] * 4,
        out_specs=[prompt_rows, per_seq(1), per_seq(CONV_WIDTH - 1)] + [sample_rows] * 4,
        out_shape=[jax.ShapeDtypeStruct((bsz, seq, d), F32),
                   jax.ShapeDtypeStruct((bsz, 1, d), F32),
                   jax.ShapeDtypeStruct((bsz, CONV_WIDTH - 1, d), F32)]
                  + [jax.ShapeDtypeStruct((ms, d), F32)] * 4,
        scratch_shapes=[pltpu.VMEM((d, win_cols), BF16), pltpu.VMEM((d, d), BF16),
                        pltpu.VMEM((d, d), BF16), pltpu.VMEM((d, d), BF16),
                        pltpu.VMEM(stage_in, F32), pltpu.VMEM(stage_sq, F32),
                        pltpu.SemaphoreType.DMA((STAGE_SLOTS,)),
                        pltpu.VMEM((HIST + tm, d), F32),
                        pltpu.VMEM((tm, d), F32),
                        pltpu.VMEM((tm, d), F32),
                        pltpu.VMEM((tm, d), BF16),
                        pltpu.VMEM((1, d), F32)],
        compiler_params=pltpu.CompilerParams(dimension_semantics=("arbitrary",), vmem_limit_bytes=vmem),
    )(xp, xs, h0, cprev, *small_args, win, pa, pb, wo)


def _block_diag_tiles(w):
    w = w.reshape(GATE_BLOCKS, HEADS_PER_BLOCK, B_HEAD_DIM, B_HEAD_DIM)
    eye = jnp.eye(HEADS_PER_BLOCK, dtype=w.dtype)
    t = jnp.einsum('bhij,hk->bhikj', w, eye)
    return t.reshape(GATE_BLOCKS, MXU_DIM, MXU_DIM)


def kernel(x_prompt, x_sample, state_lru_h, state_conv, ffn1_norm, ffn1_w_gate, ffn1_w_up, ffn1_w_down, mix_norm, w_in, gmlp_v_norm, spatial_w, spatial_b, conv_w, conv_b, lru_w_r, lru_b_r, lru_w_i, lru_b_i, lru_lambda, proj_a, proj_b, w_out, ffn2_norm, ffn2_w_gate, ffn2_w_up, ffn2_w_down, final_norm):
    depth = ffn1_norm.shape[0]
    bsz, seq, d = x_prompt.shape
    nsamp = x_sample.shape[0]
    row = lambda p: p.reshape(1, -1)
    fin = row(final_norm)

    xp = x_prompt
    xs = x_sample.reshape(nsamp, d)
    h_p, c_p, h_s, c_s, v_s = [], [], [], [], []
    for l in range(depth):
        small = {
            "mix_norm": row(mix_norm[l]), "v_norm": row(gmlp_v_norm[l]),
            "spatial_w": spatial_w[l], "spatial_b_t": spatial_b[l].T,
            "spatial_w0": row(jnp.repeat(spatial_w[l][:, 0, 0], A_GROUP_DIM)),
            "spatial_b0": row(jnp.repeat(spatial_b[l][:, 0], A_GROUP_DIM)),
            "conv_w": conv_w[l], "conv_b": row(conv_b[l]),
            "w_r": _block_diag_tiles(lru_w_r[l]).astype(BF16), "b_r": row(lru_b_r[l]),
            "w_i": _block_diag_tiles(lru_w_i[l]).astype(BF16), "b_i": row(lru_b_i[l]),
            "lam": row(lru_lambda[l]),
        }
        last = l == depth - 1

        xp2d, xs = _ffn(xp.reshape(bsz * seq, d), xs, row(ffn1_norm[l]), fin,
                        ffn1_w_gate[l].T, ffn1_w_up[l].T, ffn1_w_down[l], final_norm=False)
        cprev = jnp.swapaxes(state_conv[l], 0, 1)
        xp, hl, cn, xs, hn, xb_new, v_new = _mixer(
            xp2d.reshape(bsz, seq, d), xs, state_lru_h[l], cprev, small,
            w_in[l], proj_a[l], proj_b[l], w_out[l])
        xp2d, xs = _ffn(xp.reshape(bsz * seq, d), xs, row(ffn2_norm[l]), fin,
                        ffn2_w_gate[l].T, ffn2_w_up[l].T, ffn2_w_down[l], final_norm=last)
        xp = xp2d.reshape(bsz, seq, d)

        h_p.append(hl.reshape(bsz, d))
        c_p.append(cn)
        h_s.append(hn)
        c_s.append(jnp.concatenate([state_conv[l][:, 1:], xb_new[:, None, :]], axis=1))
        v_s.append(v_new.reshape(nsamp, 1, d))

    return (xp, xs.reshape(nsamp, 1, d), jnp.stack(h_p), jnp.stack(c_p),
            jnp.stack(h_s), jnp.stack(c_s), jnp.stack(v_s))
```

```python
import functools

import jax
import jax.numpy as jnp
from jax import lax
from jax.experimental import pallas as pl
from jax.experimental.pallas import tpu as pltpu

D_MODEL = 1024
CHUNK = 128
A_GROUPS = 8
A_GROUP_DIM = D_MODEL // A_GROUPS
B_HEADS = 16
B_HEAD_DIM = D_MODEL // B_HEADS
CONV_WIDTH = 4
LRU_C = 8.0
EPS = 1e-6

SUBLANES = 8
BF16_ROWS = 16
MXU_DIM = 256
GATE_BLOCKS = D_MODEL // MXU_DIM
HEADS_PER_BLOCK = MXU_DIM // B_HEAD_DIM

FFN_TILE = 512
MIX_TILE = 256
HIST = SUBLANES
STAGE_SLOTS = 4
STAGE_BYTES = 3 << 20

F32 = jnp.float32
BF16 = jnp.bfloat16


def _rms(x, g):
    return x * lax.rsqrt(jnp.mean(x * x, axis=-1, keepdims=True) + EPS) * g


def _sigmoid(x):
    return 0.5 * jnp.tanh(0.5 * x) + 0.5


def _resident(shape):
    nd = len(shape)
    return pl.BlockSpec(shape, lambda *_: (0,) * nd, pipeline_mode=pl.Buffered(1))


_HBM = pl.BlockSpec(memory_space=pl.ANY)


def _chunk_rows(k, n):
    best = BF16_ROWS
    for r in range(BF16_ROWS, k + 1, BF16_ROWS):
        if k % r == 0 and r * n * 4 <= STAGE_BYTES:
            best = r
    return best


def _stage_shape(k, n):
    return (STAGE_SLOTS, _chunk_rows(k, n), n)


def _stream_weights_to_bf16(jobs, sems):
    chunks = []
    for src, dst, stage in jobs:
        rows = stage.shape[1]
        for c in range(src.shape[0] // rows):
            chunks.append((src, dst, stage, c * rows, rows))

    def copy(j):
        src, _, stage, r0, rows = chunks[j]
        slot = j % STAGE_SLOTS
        return pltpu.make_async_copy(src.at[pl.ds(r0, rows), :], stage.at[slot], sems.at[slot])

    ahead = STAGE_SLOTS - 1
    for j in range(min(ahead, len(chunks))):
        copy(j).start()
    for j, (_, dst, stage, r0, rows) in enumerate(chunks):
        if j + ahead < len(chunks):
            copy(j + ahead).start()
        copy(j).wait()
        dst[r0:r0 + rows, :] = stage[j % STAGE_SLOTS].astype(BF16)


def _dot_nt(a, b_t):
    return lax.dot_general(a, b_t, (((1,), (1,)), ((), ())), preferred_element_type=F32)


def _ffn_rows(x, g_ref, wg_v, wu_v, wd_v, fg_ref, final_norm):
    h = _rms(x, g_ref[...]).astype(BF16)
    gate = _dot_nt(h, wg_v[...])
    up = _dot_nt(h, wu_v[...])
    act = (gate * _sigmoid(gate) * up).astype(BF16)
    y = x + 0.5 * jnp.dot(act, wd_v[...], preferred_element_type=F32)
    return _rms(y, fg_ref[...]) if final_norm else y


def _ffn_kernel(xp_ref, xs_ref, g_ref, fg_ref, wg_hbm, wu_hbm, wd_hbm,
                op_ref, os_ref,
                wg_v, wu_v, wd_v, stage, sems, *, n_tiles, final_norm):
    step = pl.program_id(0)

    @pl.when(step == 0)
    def _():
        _stream_weights_to_bf16([(wg_hbm, wg_v, stage), (wu_hbm, wu_v, stage), (wd_hbm, wd_v, stage)], sems)

    @pl.when(step < n_tiles)
    def _():
        op_ref[...] = _ffn_rows(xp_ref[...], g_ref, wg_v, wu_v, wd_v, fg_ref, final_norm)

    @pl.when(step == n_tiles)
    def _():
        os_ref[...] = _ffn_rows(xs_ref[...], g_ref, wg_v, wu_v, wd_v, fg_ref, final_norm)


def _ffn(xp2d, xs2d, g, fg, wg_t, wu_t, wd, *, final_norm):
    m, d = xp2d.shape
    ms = xs2d.shape[0]
    dff = wd.shape[0]
    tm = FFN_TILE
    n_tiles = m // tm
    prompt_rows = pl.BlockSpec((tm, d), lambda i: (jnp.minimum(i, n_tiles - 1), 0))
    sample_rows = pl.BlockSpec((ms, d), lambda i: (0, 0))
    stage = _stage_shape(dff, d)
    vmem = (2 * 3 * d * dff
            + 4 * stage[0] * stage[1] * stage[2]
            + 4 * 4 * (tm + ms) * d
            + tm * (2 * 4 + 2 + 2) * dff + 4 * 4 * tm * d)
    return pl.pallas_call(
        functools.partial(_ffn_kernel, n_tiles=n_tiles, final_norm=final_norm),
        grid=(n_tiles + 1,),
        in_specs=[prompt_rows, sample_rows, _resident(g.shape), _resident(fg.shape), _HBM, _HBM, _HBM],
        out_specs=[prompt_rows, sample_rows],
        out_shape=[jax.ShapeDtypeStruct((m, d), F32), jax.ShapeDtypeStruct((ms, d), F32)],
        scratch_shapes=[pltpu.VMEM((dff, d), BF16)] * 3
                       + [pltpu.VMEM(stage, F32), pltpu.SemaphoreType.DMA((STAGE_SLOTS,))],
        compiler_params=pltpu.CompilerParams(dimension_semantics=("arbitrary",), vmem_limit_bytes=vmem),
    )(xp2d, xs2d, g, fg, wg_t, wu_t, wd)


def _in_proj(h, win_v, k):
    return jnp.dot(h, win_v[:, k * D_MODEL:(k + 1) * D_MODEL], preferred_element_type=F32)


def _lru_coeffs(xc, wr_ref, br_ref, wi_ref, bi_ref, lam_ref):
    xcb = xc.astype(BF16)
    r_parts, i_parts = [], []
    for j in range(GATE_BLOCKS):
        blk = xcb[:, j * MXU_DIM:(j + 1) * MXU_DIM]
        r_parts.append(jnp.dot(blk, wr_ref[j], preferred_element_type=F32))
        i_parts.append(jnp.dot(blk, wi_ref[j], preferred_element_type=F32))
    lam = lam_ref[...]
    half_rate = (-0.5 * LRU_C) * (jnp.maximum(-lam, 0.0) + jnp.log1p(jnp.exp(-jnp.abs(lam))))
    t_r = jnp.tanh(0.5 * jnp.concatenate(r_parts, axis=-1) + 0.5 * br_ref[...])
    t_i = jnp.tanh(0.5 * jnp.concatenate(i_parts, axis=-1) + 0.5 * bi_ref[...])
    log_a = t_r * half_rate + half_rate
    a = jnp.exp(log_a)
    th = jnp.tanh(log_a)
    num = -2.0 * th
    root = jnp.where(num == 0.0, 0.0, num * lax.rsqrt(num)) * lax.rsqrt(1.0 - th)
    half_xc = 0.5 * xc
    b = root * (t_i * half_xc + half_xc)
    return a, b


def _merge(x, ya_b16, yb, g_a, g_b, pa_v, pb_v, wo_v):
    m = (_sigmoid(g_a) * jnp.dot(ya_b16, pa_v[...], preferred_element_type=F32)
         + _sigmoid(g_b) * jnp.dot(yb.astype(BF16), pb_v[...], preferred_element_type=F32))
    return x + jnp.dot(m.astype(BF16), wo_v[...], preferred_element_type=F32)


def _mixer_prompt_tile(x_ref, p, win_v, pa_v, pb_v, wo_v, xo_ref, hlast_ref, convnew_ref,
                       xb_scr, a_scr, b_scr, ya_scr, h_scr, first_tile):
    tm = x_ref.shape[0]

    @pl.when(first_tile)
    def _():
        xb_scr[...] = jnp.zeros((HIST, D_MODEL), F32)
        h_scr[...] = jnp.zeros((1, D_MODEL), F32)

    x = x_ref[...]
    h = _rms(x, p["mix_norm"][...]).astype(BF16)

    u = jax.nn.gelu(_in_proj(h, win_v, 0))
    v = _rms(jax.nn.gelu(_in_proj(h, win_v, 1)), p["v_norm"][...]).astype(BF16)
    causal = (lax.broadcasted_iota(jnp.int32, (CHUNK, CHUNK), 0)
              >= lax.broadcasted_iota(jnp.int32, (CHUNK, CHUNK), 1))
    for g in range(A_GROUPS):
        w_g = jnp.where(causal, p["spatial_w"][g], 0.0).astype(BF16)
        bias_g = p["spatial_b_t"][:, g:g + 1]
        cols = slice(g * A_GROUP_DIM, (g + 1) * A_GROUP_DIM)
        for c in range(tm // CHUNK):
            rows = slice(c * CHUNK, (c + 1) * CHUNK)
            s = jnp.dot(w_g, v[rows, cols], preferred_element_type=F32) + bias_g
            ya_scr[rows, cols] = (u[rows, cols] * s).astype(BF16)

    xb = _in_proj(h, win_v, 2)
    prev = xb_scr[...]
    cw = p["conv_w"]
    xc = p["conv_b"][...] + cw[CONV_WIDTH - 1:CONV_WIDTH, :] * xb
    for k in range(CONV_WIDTH - 1):
        back = CONV_WIDTH - 1 - k
        shifted = jnp.concatenate([prev[HIST - back:, :], xb[:tm - back, :]], axis=0)
        xc = xc + cw[k:k + 1, :] * shifted
    convnew_ref[...] = xb[tm - (CONV_WIDTH - 1):, :]
    xb_scr[...] = xb[tm - HIST:, :]

    a, b = _lru_coeffs(xc, p["w_r"], p["b_r"], p["w_i"], p["b_i"], p["lam"])
    a_scr[...] = a
    b_scr[...] = b

    first_row = lax.broadcasted_iota(jnp.int32, (SUBLANES, D_MODEL), 0) == 0
    carry = h_scr[...]
    for r0 in range(0, tm, SUBLANES):
        ag = a_scr[r0:r0 + SUBLANES, :]
        hg = jnp.where(first_row, ag, 0.0) * carry + b_scr[r0:r0 + SUBLANES, :]
        span = jnp.where(first_row, 0.0, ag)
        d = 1
        while d < SUBLANES:
            hg = span * pltpu.roll(hg, d, 0) + hg
            if 2 * d < SUBLANES:
                span = span * pltpu.roll(span, d, 0)
            d *= 2
        b_scr[r0:r0 + SUBLANES, :] = hg
        carry = hg[SUBLANES - 1:SUBLANES, :]
    h_scr[...] = carry
    hlast_ref[...] = carry

    yb = b_scr[...] * jax.nn.gelu(_in_proj(h, win_v, 3))
    xo_ref[...] = _merge(x, ya_scr[...], yb, _in_proj(h, win_v, 4), _in_proj(h, win_v, 5),
                         pa_v, pb_v, wo_v)


def _mixer_sample_rows(x_ref, h0_ref, cprev_ref, p, win_v, pa_v, pb_v, wo_v,
                       xo_ref, hnew_ref, xbnew_ref, v_ref):
    x = x_ref[...]
    h = _rms(x, p["mix_norm"][...]).astype(BF16)

    u = jax.nn.gelu(_in_proj(h, win_v, 0))
    v = _rms(jax.nn.gelu(_in_proj(h, win_v, 1)), p["v_norm"][...])
    v_ref[...] = v
    ya = u * (p["spatial_w0"][...] * v + p["spatial_b0"][...])

    xb = _in_proj(h, win_v, 2)
    xbnew_ref[...] = xb
    cw = p["conv_w"]
    xc = p["conv_b"][...] + cw[CONV_WIDTH - 1:CONV_WIDTH, :] * xb
    for k in range(CONV_WIDTH - 1):
        xc = xc + cw[k:k + 1, :] * cprev_ref[k]

    a, b = _lru_coeffs(xc, p["w_r"], p["b_r"], p["w_i"], p["b_i"], p["lam"])
    h_new = a * h0_ref[...] + b
    hnew_ref[...] = h_new

    yb = h_new * jax.nn.gelu(_in_proj(h, win_v, 3))
    xo_ref[...] = _merge(x, ya.astype(BF16), yb, _in_proj(h, win_v, 4), _in_proj(h, win_v, 5),
                         pa_v, pb_v, wo_v)


_SMALL_PARAMS = ("mix_norm", "v_norm", "spatial_w", "spatial_b_t", "spatial_w0", "spatial_b0",
                 "conv_w", "conv_b", "w_r", "b_r", "w_i", "b_i", "lam")


def _mixer_kernel(*refs, n_tiles, tiles_per_seq):
    n_small = len(_SMALL_PARAMS)
    xp_ref, xs_ref, h0_ref, cprev_ref = refs[:4]
    p = dict(zip(_SMALL_PARAMS, refs[4:4 + n_small]))
    win_hbm, pa_hbm, pb_hbm, wo_hbm = refs[4 + n_small:8 + n_small]
    xo_p, hlast_ref, convnew_ref, xo_s, hnew_ref, xbnew_ref, v_ref = refs[8 + n_small:15 + n_small]
    (win_v, pa_v, pb_v, wo_v, stage_in, stage_sq, sems,
     xb_scr, a_scr, b_scr, ya_scr, h_scr) = refs[15 + n_small:]
    step = pl.program_id(0)

    @pl.when(step == 0)
    def _():
        _stream_weights_to_bf16([(win_hbm, win_v, stage_in), (pa_hbm, pa_v, stage_sq),
                                 (pb_hbm, pb_v, stage_sq), (wo_hbm, wo_v, stage_sq)], sems)

    @pl.when(step < n_tiles)
    def _():
        _mixer_prompt_tile(xp_ref, p, win_v, pa_v, pb_v, wo_v, xo_p, hlast_ref, convnew_ref,
                           xb_scr, a_scr, b_scr, ya_scr, h_scr, step % tiles_per_seq == 0)

    @pl.when(step == n_tiles)
    def _():
        _mixer_sample_rows(xs_ref, h0_ref, cprev_ref, p, win_v, pa_v, pb_v, wo_v,
                           xo_s, hnew_ref, xbnew_ref, v_ref)


def _mixer(xp, xs, h0, cprev, small, win, pa, pb, wo):
    bsz, seq, d = xp.shape
    ms = xs.shape[0]
    tm = MIX_TILE
    tiles_per_seq = seq // tm
    n_tiles = bsz * tiles_per_seq
    win_cols = win.shape[1]

    def tile_of(i):
        j = jnp.minimum(i, n_tiles - 1)
        return j // tiles_per_seq, j % tiles_per_seq

    prompt_rows = pl.BlockSpec((None, tm, d), lambda i: (*tile_of(i), 0))
    per_seq = lambda rows: pl.BlockSpec((None, rows, d), lambda i: (tile_of(i)[0], 0, 0))
    sample_rows = pl.BlockSpec((ms, d), lambda i: (0, 0))
    small_args = [small[k] for k in _SMALL_PARAMS]
    stage_in, stage_sq = _stage_shape(d, win_cols), _stage_shape(d, d)
    vmem = (2 * d * (win_cols + 3 * d)
            + 4 * (stage_in[0] * stage_in[1] * stage_in[2] + stage_sq[0] * stage_sq[1] * stage_sq[2])
            + sum(a.size * a.dtype.itemsize for a in small_args) * 2
            + 4 * 4 * tm * d + 4 * 16 * ms * d
            + (4 * 2 + 2) * tm * d + 4 * HIST * d
            + 4 * 4 * tm * d)
    return pl.pallas_call(
        functools.partial(_mixer_kernel, n_tiles=n_tiles, tiles_per_seq=tiles_per_seq),
        grid=(n_tiles + 1,),
        in_specs=[prompt_rows, sample_rows, sample_rows, _resident(cprev.shape)]
                 + [_resident(a.shape) for a in small_args] + [_HBM] * 4,
        out_specs=[prompt_rows, per_seq(1), per_seq(CONV_WIDTH - 1)] + [sample_rows] * 4,
        out_shape=[jax.ShapeDtypeStruct((bsz, seq, d), F32),
                   jax.ShapeDtypeStruct((bsz, 1, d), F32),
                   jax.ShapeDtypeStruct((bsz, CONV_WIDTH - 1, d), F32)]
                  + [jax.ShapeDtypeStruct((ms, d), F32)] * 4,
        scratch_shapes=[pltpu.VMEM((d, win_cols), BF16), pltpu.VMEM((d, d), BF16),
                        pltpu.VMEM((d, d), BF16), pltpu.VMEM((d, d), BF16),
                        pltpu.VMEM(stage_in, F32), pltpu.VMEM(stage_sq, F32),
                        pltpu.SemaphoreType.DMA((STAGE_SLOTS,)),
                        pltpu.VMEM((HIST, d), F32),
                        pltpu.VMEM((tm, d), F32),
                        pltpu.VMEM((tm, d), F32),
                        pltpu.VMEM((tm, d), BF16),
                        pltpu.VMEM((1, d), F32)],
        compiler_params=pltpu.CompilerParams(dimension_semantics=("arbitrary",), vmem_limit_bytes=vmem),
    )(xp, xs, h0, cprev, *small_args, win, pa, pb, wo)


def _block_diag_tiles(w):
    w = w.reshape(GATE_BLOCKS, HEADS_PER_BLOCK, B_HEAD_DIM, B_HEAD_DIM)
    eye = jnp.eye(HEADS_PER_BLOCK, dtype=w.dtype)
    t = jnp.einsum('bhij,hk->bhikj', w, eye)
    return t.reshape(GATE_BLOCKS, MXU_DIM, MXU_DIM)


def kernel(x_prompt, x_sample, state_lru_h, state_conv, ffn1_norm, ffn1_w_gate, ffn1_w_up, ffn1_w_down, mix_norm, w_in, gmlp_v_norm, spatial_w, spatial_b, conv_w, conv_b, lru_w_r, lru_b_r, lru_w_i, lru_b_i, lru_lambda, proj_a, proj_b, w_out, ffn2_norm, ffn2_w_gate, ffn2_w_up, ffn2_w_down, final_norm):
    depth = ffn1_norm.shape[0]
    bsz, seq, d = x_prompt.shape
    nsamp = x_sample.shape[0]
    row = lambda p: p.reshape(1, -1)
    fin = row(final_norm)

    xp = x_prompt
    xs = x_sample.reshape(nsamp, d)
    h_p, c_p, h_s, c_s, v_s = [], [], [], [], []
    for l in range(depth):
        small = {
            "mix_norm": row(mix_norm[l]), "v_norm": row(gmlp_v_norm[l]),
            "spatial_w": spatial_w[l], "spatial_b_t": spatial_b[l].T,
            "spatial_w0": row(jnp.repeat(spatial_w[l][:, 0, 0], A_GROUP_DIM)),
            "spatial_b0": row(jnp.repeat(spatial_b[l][:, 0], A_GROUP_DIM)),
            "conv_w": conv_w[l], "conv_b": row(conv_b[l]),
            "w_r": _block_diag_tiles(lru_w_r[l]).astype(BF16), "b_r": row(lru_b_r[l]),
            "w_i": _block_diag_tiles(lru_w_i[l]).astype(BF16), "b_i": row(lru_b_i[l]),
            "lam": row(lru_lambda[l]),
        }
        last = l == depth - 1

        xp2d, xs = _ffn(xp.reshape(bsz * seq, d), xs, row(ffn1_norm[l]), fin,
                        ffn1_w_gate[l].T, ffn1_w_up[l].T, ffn1_w_down[l], final_norm=False)
        cprev = jnp.swapaxes(state_conv[l], 0, 1)
        xp, hl, cn, xs, hn, xb_new, v_new = _mixer(
            xp2d.reshape(bsz, seq, d), xs, state_lru_h[l], cprev, small,
            w_in[l], proj_a[l], proj_b[l], w_out[l])
        xp2d, xs = _ffn(xp.reshape(bsz * seq, d), xs, row(ffn2_norm[l]), fin,
                        ffn2_w_gate[l].T, ffn2_w_up[l].T, ffn2_w_down[l], final_norm=last)
        xp = xp2d.reshape(bsz, seq, d)

        h_p.append(hl.reshape(bsz, d))
        c_p.append(cn)
        h_s.append(hn)
        c_s.append(jnp.concatenate([state_conv[l][:, 1:], xb_new[:, None, :]], axis=1))
        v_s.append(v_new.reshape(nsamp, 1, d))

    return (xp, xs.reshape(nsamp, 1, d), jnp.stack(h_p), jnp.stack(c_p),
            jnp.stack(h_s), jnp.stack(c_s), jnp.stack(v_s))
```

```python
import functools

import jax
import jax.numpy as jnp
from jax import lax
from jax.experimental import pallas as pl
from jax.experimental.pallas import tpu as pltpu

D_MODEL = 1024
CHUNK = 128
A_GROUPS = 8
A_GROUP_DIM = D_MODEL // A_GROUPS
B_HEADS = 16
B_HEAD_DIM = D_MODEL // B_HEADS
CONV_WIDTH = 4
LRU_C = 8.0
EPS = 1e-6

SUBLANES = 8
BF16_ROWS = 16
MXU_DIM = 256
GATE_BLOCKS = D_MODEL // MXU_DIM
HEADS_PER_BLOCK = MXU_DIM // B_HEAD_DIM

FFN_TILE = 1024
FFN_HIDDEN_PARTS = 2
MIX_TILE = 256
HIST = SUBLANES
STAGE_SLOTS = 2
STAGE_BYTES = 3 << 20

F32 = jnp.float32
BF16 = jnp.bfloat16


def _rms(x, g):
    return x * lax.rsqrt(jnp.mean(x * x, axis=-1, keepdims=True) + EPS) * g


def _sigmoid(x):
    return 0.5 * jnp.tanh(0.5 * x) + 0.5


def _resident(shape):
    nd = len(shape)
    return pl.BlockSpec(shape, lambda *_: (0,) * nd, pipeline_mode=pl.Buffered(1))


_HBM = pl.BlockSpec(memory_space=pl.ANY)


def _chunk_rows(k, n):
    best = BF16_ROWS
    for r in range(BF16_ROWS, k + 1, BF16_ROWS):
        if k % r == 0 and r * n * 4 <= STAGE_BYTES:
            best = r
    return best


def _stage_shape(k, n):
    return (STAGE_SLOTS, _chunk_rows(k, n), n)


def _stream_weights_to_bf16(jobs, sems):
    chunks = []
    for src, dst, stage in jobs:
        rows = stage.shape[1]
        for c in range(src.shape[0] // rows):
            chunks.append((src, dst, stage, c * rows, rows))

    def copy(j):
        src, _, stage, r0, rows = chunks[j]
        slot = j % STAGE_SLOTS
        return pltpu.make_async_copy(src.at[pl.ds(r0, rows), :], stage.at[slot], sems.at[slot])

    ahead = STAGE_SLOTS - 1
    for j in range(min(ahead, len(chunks))):
        copy(j).start()
    for j, (_, dst, stage, r0, rows) in enumerate(chunks):
        if j + ahead < len(chunks):
            copy(j + ahead).start()
        copy(j).wait()
        dst[r0:r0 + rows, :] = stage[j % STAGE_SLOTS].astype(BF16)


def _dot_nt(a, b_t):
    return lax.dot_general(a, b_t, (((1,), (1,)), ((), ())), preferred_element_type=F32)


def _ffn_rows(x, g_ref, wg_v, wu_v, wd_v, fg_ref, final_norm):
    h = _rms(x, g_ref[...]).astype(BF16)
    dff = wd_v.shape[0]
    cut = (dff // FFN_HIDDEN_PARTS) // MXU_DIM * MXU_DIM
    bounds = [i * cut for i in range(FFN_HIDDEN_PARTS)] + [dff]
    down = None
    for lo, hi in zip(bounds[:-1], bounds[1:]):
        gate = _dot_nt(h, wg_v[lo:hi, :])
        up = _dot_nt(h, wu_v[lo:hi, :])
        act = (gate * _sigmoid(gate) * up).astype(BF16)
        part = jnp.dot(act, wd_v[lo:hi, :], preferred_element_type=F32)
        down = part if down is None else down + part
    y = x + 0.5 * down
    return _rms(y, fg_ref[...]) if final_norm else y


def _ffn_kernel(xp_ref, xs_ref, g_ref, fg_ref, wg_hbm, wu_hbm, wd_hbm,
                op_ref, os_ref,
                wg_v, wu_v, wd_v, stage, sems, *, n_tiles, final_norm):
    step = pl.program_id(0)

    @pl.when(step == 0)
    def _():
        _stream_weights_to_bf16([(wg_hbm, wg_v, stage), (wu_hbm, wu_v, stage), (wd_hbm, wd_v, stage)], sems)

    @pl.when(step < n_tiles)
    def _():
        op_ref[...] = _ffn_rows(xp_ref[...], g_ref, wg_v, wu_v, wd_v, fg_ref, final_norm)

    @pl.when(step == n_tiles)
    def _():
        os_ref[...] = _ffn_rows(xs_ref[...], g_ref, wg_v, wu_v, wd_v, fg_ref, final_norm)


def _ffn(xp2d, xs2d, g, fg, wg_t, wu_t, wd, *, final_norm):
    m, d = xp2d.shape
    ms = xs2d.shape[0]
    dff = wd.shape[0]
    tm = FFN_TILE
    n_tiles = m // tm
    prompt_rows = pl.BlockSpec((tm, d), lambda i: (jnp.minimum(i, n_tiles - 1), 0))
    sample_rows = pl.BlockSpec((ms, d), lambda i: (0, 0))
    stage = _stage_shape(dff, d)
    vmem = (2 * 3 * d * dff
            + 4 * stage[0] * stage[1] * stage[2]
            + 4 * 4 * (tm + ms) * d
            + tm * (2 * 4 + 2) * dff // FFN_HIDDEN_PARTS
            + (2 + 4 * 4) * tm * d)
    return pl.pallas_call(
        functools.partial(_ffn_kernel, n_tiles=n_tiles, final_norm=final_norm),
        grid=(n_tiles + 1,),
        in_specs=[prompt_rows, sample_rows, _resident(g.shape), _resident(fg.shape), _HBM, _HBM, _HBM],
        out_specs=[prompt_rows, sample_rows],
        out_shape=[jax.ShapeDtypeStruct((m, d), F32), jax.ShapeDtypeStruct((ms, d), F32)],
        scratch_shapes=[pltpu.VMEM((dff, d), BF16)] * 3
                       + [pltpu.VMEM(stage, F32), pltpu.SemaphoreType.DMA((STAGE_SLOTS,))],
        compiler_params=pltpu.CompilerParams(dimension_semantics=("arbitrary",), vmem_limit_bytes=vmem),
    )(xp2d, xs2d, g, fg, wg_t, wu_t, wd)


def _in_proj(h, win_v, k):
    return jnp.dot(h, win_v[:, k * D_MODEL:(k + 1) * D_MODEL], preferred_element_type=F32)


def _lru_coeffs(xc, wr_ref, br_ref, wi_ref, bi_ref, lam_ref):
    xcb = xc.astype(BF16)
    r_parts, i_parts = [], []
    for j in range(GATE_BLOCKS):
        blk = xcb[:, j * MXU_DIM:(j + 1) * MXU_DIM]
        r_parts.append(jnp.dot(blk, wr_ref[j], preferred_element_type=F32))
        i_parts.append(jnp.dot(blk, wi_ref[j], preferred_element_type=F32))
    lam = lam_ref[...]
    half_rate = (-0.5 * LRU_C) * (jnp.maximum(-lam, 0.0) + jnp.log1p(jnp.exp(-jnp.abs(lam))))
    t_r = jnp.tanh(0.5 * jnp.concatenate(r_parts, axis=-1) + 0.5 * br_ref[...])
    t_i = jnp.tanh(0.5 * jnp.concatenate(i_parts, axis=-1) + 0.5 * bi_ref[...])
    log_a = t_r * half_rate + half_rate
    a = jnp.exp(log_a)
    th = jnp.tanh(log_a)
    num = -2.0 * th
    root = jnp.where(num == 0.0, 0.0, num * lax.rsqrt(num)) * lax.rsqrt(1.0 - th)
    half_xc = 0.5 * xc
    b = root * (t_i * half_xc + half_xc)
    return a, b


def _merge(x, ya_b16, yb, g_a, g_b, pa_v, pb_v, wo_v):
    m = (_sigmoid(g_a) * jnp.dot(ya_b16, pa_v[...], preferred_element_type=F32)
         + _sigmoid(g_b) * jnp.dot(yb.astype(BF16), pb_v[...], preferred_element_type=F32))
    return x + jnp.dot(m.astype(BF16), wo_v[...], preferred_element_type=F32)


def _mixer_prompt_tile(x_ref, p, win_v, pa_v, pb_v, wo_v, xo_ref, hlast_ref, convnew_ref,
                       xb_scr, a_scr, b_scr, ya_scr, h_scr, first_tile):
    tm = x_ref.shape[0]

    @pl.when(first_tile)
    def _():
        xb_scr[0:HIST, :] = jnp.zeros((HIST, D_MODEL), F32)
        h_scr[...] = jnp.zeros((1, D_MODEL), F32)

    x = x_ref[...]
    h = _rms(x, p["mix_norm"][...]).astype(BF16)

    u = jax.nn.gelu(_in_proj(h, win_v, 0))
    v = _rms(jax.nn.gelu(_in_proj(h, win_v, 1)), p["v_norm"][...]).astype(BF16)
    causal = (lax.broadcasted_iota(jnp.int32, (CHUNK, CHUNK), 0)
              >= lax.broadcasted_iota(jnp.int32, (CHUNK, CHUNK), 1))
    for g in range(A_GROUPS):
        w_g = jnp.where(causal, p["spatial_w"][g], 0.0).astype(BF16)
        bias_g = p["spatial_b_t"][:, g:g + 1]
        cols = slice(g * A_GROUP_DIM, (g + 1) * A_GROUP_DIM)
        for c in range(tm // CHUNK):
            rows = slice(c * CHUNK, (c + 1) * CHUNK)
            s = jnp.dot(w_g, v[rows, cols], preferred_element_type=F32) + bias_g
            ya_scr[rows, cols] = (u[rows, cols] * s).astype(BF16)

    xb = _in_proj(h, win_v, 2)
    xb_scr[HIST:HIST + tm, :] = xb
    cw = p["conv_w"]
    xc = p["conv_b"][...] + cw[CONV_WIDTH - 1:CONV_WIDTH, :] * xb
    for k in range(CONV_WIDTH - 1):
        back = CONV_WIDTH - 1 - k
        xc = xc + cw[k:k + 1, :] * xb_scr[HIST - back:HIST - back + tm, :]
    convnew_ref[...] = xb_scr[HIST + tm - (CONV_WIDTH - 1):HIST + tm, :]
    xb_scr[0:HIST, :] = xb_scr[tm:tm + HIST, :]

    a, b = _lru_coeffs(xc, p["w_r"], p["b_r"], p["w_i"], p["b_i"], p["lam"])
    a_scr[...] = a
    b_scr[...] = b

    first_row = lax.broadcasted_iota(jnp.int32, (SUBLANES, D_MODEL), 0) == 0
    carry = h_scr[...]
    for r0 in range(0, tm, SUBLANES):
        ag = a_scr[r0:r0 + SUBLANES, :]
        hg = jnp.where(first_row, ag, 0.0) * carry + b_scr[r0:r0 + SUBLANES, :]
        span = jnp.where(first_row, 0.0, ag)
        d = 1
        while d < SUBLANES:
            hg = span * pltpu.roll(hg, d, 0) + hg
            if 2 * d < SUBLANES:
                span = span * pltpu.roll(span, d, 0)
            d *= 2
        b_scr[r0:r0 + SUBLANES, :] = hg
        carry = hg[SUBLANES - 1:SUBLANES, :]
    h_scr[...] = carry
    hlast_ref[...] = carry

    yb = b_scr[...] * jax.nn.gelu(_in_proj(h, win_v, 3))
    xo_ref[...] = _merge(x, ya_scr[...], yb, _in_proj(h, win_v, 4), _in_proj(h, win_v, 5),
                         pa_v, pb_v, wo_v)


def _mixer_sample_rows(x_ref, h0_ref, cprev_ref, p, win_v, pa_v, pb_v, wo_v,
                       xo_ref, hnew_ref, xbnew_ref, v_ref):
    x = x_ref[...]
    h = _rms(x, p["mix_norm"][...]).astype(BF16)

    u = jax.nn.gelu(_in_proj(h, win_v, 0))
    v = _rms(jax.nn.gelu(_in_proj(h, win_v, 1)), p["v_norm"][...])
    v_ref[...] = v
    ya = u * (p["spatial_w0"][...] * v + p["spatial_b0"][...])

    xb = _in_proj(h, win_v, 2)
    xbnew_ref[...] = xb
    cw = p["conv_w"]
    xc = p["conv_b"][...] + cw[CONV_WIDTH - 1:CONV_WIDTH, :] * xb
    for k in range(CONV_WIDTH - 1):
        xc = xc + cw[k:k + 1, :] * cprev_ref[k]

    a, b = _lru_coeffs(xc, p["w_r"], p["b_r"], p["w_i"], p["b_i"], p["lam"])
    h_new = a * h0_ref[...] + b
    hnew_ref[...] = h_new

    yb = h_new * jax.nn.gelu(_in_proj(h, win_v, 3))
    xo_ref[...] = _merge(x, ya.astype(BF16), yb, _in_proj(h, win_v, 4), _in_proj(h, win_v, 5),
                         pa_v, pb_v, wo_v)


_SMALL_PARAMS = ("mix_norm", "v_norm", "spatial_w", "spatial_b_t", "spatial_w0", "spatial_b0",
                 "conv_w", "conv_b", "w_r", "b_r", "w_i", "b_i", "lam")


def _mixer_kernel(*refs, n_tiles, tiles_per_seq):
    n_small = len(_SMALL_PARAMS)
    xp_ref, xs_ref, h0_ref, cprev_ref = refs[:4]
    p = dict(zip(_SMALL_PARAMS, refs[4:4 + n_small]))
    win_hbm, pa_hbm, pb_hbm, wo_hbm = refs[4 + n_small:8 + n_small]
    xo_p, hlast_ref, convnew_ref, xo_s, hnew_ref, xbnew_ref, v_ref = refs[8 + n_small:15 + n_small]
    (win_v, pa_v, pb_v, wo_v, stage_in, stage_sq, sems,
     xb_scr, a_scr, b_scr, ya_scr, h_scr) = refs[15 + n_small:]
    step = pl.program_id(0)

    @pl.when(step == 0)
    def _():
        _stream_weights_to_bf16([(win_hbm, win_v, stage_in), (pa_hbm, pa_v, stage_sq),
                                 (pb_hbm, pb_v, stage_sq), (wo_hbm, wo_v, stage_sq)], sems)

    @pl.when(step < n_tiles)
    def _():
        _mixer_prompt_tile(xp_ref, p, win_v, pa_v, pb_v, wo_v, xo_p, hlast_ref, convnew_ref,
                           xb_scr, a_scr, b_scr, ya_scr, h_scr, step % tiles_per_seq == 0)

    @pl.when(step == n_tiles)
    def _():
        _mixer_sample_rows(xs_ref, h0_ref, cprev_ref, p, win_v, pa_v, pb_v, wo_v,
                           xo_s, hnew_ref, xbnew_ref, v_ref)


def _mixer(xp, xs, h0, cprev, small, win, pa, pb, wo):
    bsz, seq, d = xp.shape
    ms = xs.shape[0]
    tm = MIX_TILE
    tiles_per_seq = seq // tm
    n_tiles = bsz * tiles_per_seq
    win_cols = win.shape[1]

    def tile_of(i):
        j = jnp.minimum(i, n_tiles - 1)
        return j // tiles_per_seq, j % tiles_per_seq

    prompt_rows = pl.BlockSpec((None, tm, d), lambda i: (*tile_of(i), 0))
    per_seq = lambda rows: pl.BlockSpec((None, rows, d), lambda i: (tile_of(i)[0], 0, 0))
    sample_rows = pl.BlockSpec((ms, d), lambda i: (0, 0))
    small_args = [small[k] for k in _SMALL_PARAMS]
    stage_in, stage_sq = _stage_shape(d, win_cols), _stage_shape(d, d)
    vmem = (2 * d * (win_cols + 3 * d)
            + 4 * (stage_in[0] * stage_in[1] * stage_in[2] + stage_sq[0] * stage_sq[1] * stage_sq[2])
            + sum(a.size * a.dtype.itemsize for a in small_args) * 2
            + 4 * 4 * tm * d + 4 * 16 * ms * d
            + (4 * 3 + 2) * tm * d + 4 * HIST * d
            + 4 * 4 * tm * d)
    return pl.pallas_call(
        functools.partial(_mixer_kernel, n_tiles=n_tiles, tiles_per_seq=tiles_per_seq),
        grid=(n_tiles + 1,),
        in_specs=[prompt_rows, sample_rows, sample_rows, _resident(cprev.shape)]
                 + [_resident(a.shape) for a in small_args] + [_HBM] * 4,
        out_specs=[prompt_rows, per_seq(1), per_seq(CONV_WIDTH - 1)] + [sample_rows] * 4,
        out_shape=[jax.ShapeDtypeStruct((bsz, seq, d), F32),
                   jax.ShapeDtypeStruct((bsz, 1, d), F32),
                   jax.ShapeDtypeStruct((bsz, CONV_WIDTH - 1, d), F32)]
                  + [jax.ShapeDtypeStruct((ms, d), F32)] * 4,
        scratch_shapes=[pltpu.VMEM((d, win_cols), BF16), pltpu.VMEM((d, d), BF16),
                        pltpu.VMEM((d, d), BF16), pltpu.VMEM((d, d), BF16),
                        pltpu.VMEM(stage_in, F32), pltpu.VMEM(stage_sq, F32),
                        pltpu.SemaphoreType.DMA((STAGE_SLOTS,)),
                        pltpu.VMEM((HIST + tm, d), F32),
                        pltpu.VMEM((tm, d), F32),
                        pltpu.VMEM((tm, d), F32),
                        pltpu.VMEM((tm, d), BF16),
                        pltpu.VMEM((1, d), F32)],
        compiler_params=pltpu.CompilerParams(dimension_semantics=("arbitrary",), vmem_limit_bytes=vmem),
    )(xp, xs, h0, cprev, *small_args, win, pa, pb, wo)


def _block_diag_tiles(w):
    w = w.reshape(GATE_BLOCKS, HEADS_PER_BLOCK, B_HEAD_DIM, B_HEAD_DIM)
    eye = jnp.eye(HEADS_PER_BLOCK, dtype=w.dtype)
    t = jnp.einsum('bhij,hk->bhikj', w, eye)
    return t.reshape(GATE_BLOCKS, MXU_DIM, MXU_DIM)


def kernel(x_prompt, x_sample, state_lru_h, state_conv, ffn1_norm, ffn1_w_gate, ffn1_w_up, ffn1_w_down, mix_norm, w_in, gmlp_v_norm, spatial_w, spatial_b, conv_w, conv_b, lru_w_r, lru_b_r, lru_w_i, lru_b_i, lru_lambda, proj_a, proj_b, w_out, ffn2_norm, ffn2_w_gate, ffn2_w_up, ffn2_w_down, final_norm):
    depth = ffn1_norm.shape[0]
    bsz, seq, d = x_prompt.shape
    nsamp = x_sample.shape[0]
    row = lambda p: p.reshape(1, -1)
    fin = row(final_norm)

    xp = x_prompt
    xs = x_sample.reshape(nsamp, d)
    h_p, c_p, h_s, c_s, v_s = [], [], [], [], []
    for l in range(depth):
        small = {
            "mix_norm": row(mix_norm[l]), "v_norm": row(gmlp_v_norm[l]),
            "spatial_w": spatial_w[l], "spatial_b_t": spatial_b[l].T,
            "spatial_w0": row(jnp.repeat(spatial_w[l][:, 0, 0], A_GROUP_DIM)),
            "spatial_b0": row(jnp.repeat(spatial_b[l][:, 0], A_GROUP_DIM)),
            "conv_w": conv_w[l], "conv_b": row(conv_b[l]),
            "w_r": _block_diag_tiles(lru_w_r[l]).astype(BF16), "b_r": row(lru_b_r[l]),
            "w_i": _block_diag_tiles(lru_w_i[l]).astype(BF16), "b_i": row(lru_b_i[l]),
            "lam": row(lru_lambda[l]),
        }
        last = l == depth - 1

        xp2d, xs = _ffn(xp.reshape(bsz * seq, d), xs, row(ffn1_norm[l]), fin,
                        ffn1_w_gate[l].T, ffn1_w_up[l].T, ffn1_w_down[l], final_norm=False)
        cprev = jnp.swapaxes(state_conv[l], 0, 1)
        xp, hl, cn, xs, hn, xb_new, v_new = _mixer(
            xp2d.reshape(bsz, seq, d), xs, state_lru_h[l], cprev, small,
            w_in[l], proj_a[l], proj_b[l], w_out[l])
        xp2d, xs = _ffn(xp.reshape(bsz * seq, d), xs, row(ffn2_norm[l]), fin,
                        ffn2_w_gate[l].T, ffn2_w_up[l].T, ffn2_w_down[l], final_norm=last)
        xp = xp2d.reshape(bsz, seq, d)

        h_p.append(hl.reshape(bsz, d))
        c_p.append(cn)
        h_s.append(hn)
        c_s.append(jnp.concatenate([state_conv[l][:, 1:], xb_new[:, None, :]], axis=1))
        v_s.append(v_new.reshape(nsamp, 1, d))

    return (xp, xs.reshape(nsamp, 1, d), jnp.stack(h_p), jnp.stack(c_p),
            jnp.stack(h_s), jnp.stack(c_s), jnp.stack(v_s))
```

```python
import functools

import jax
import jax.numpy as jnp
from jax import lax
from jax.experimental import pallas as pl
from jax.experimental.pallas import tpu as pltpu

D_MODEL = 1024
CHUNK = 128
A_GROUPS = 8
A_GROUP_DIM = D_MODEL // A_GROUPS
B_HEADS = 16
B_HEAD_DIM = D_MODEL // B_HEADS
CONV_WIDTH = 4
LRU_C = 8.0
EPS = 1e-6

SUBLANES = 8
BF16_ROWS = 16
MXU_DIM = 256
GATE_BLOCKS = D_MODEL // MXU_DIM
HEADS_PER_BLOCK = MXU_DIM // B_HEAD_DIM

FFN_TILE = 1024
FFN_HIDDEN_PARTS = 2
MIX_TILE = 512
MIX_SUB = 256
HIST = SUBLANES
STAGE_SLOTS = 2
STAGE_BYTES = 3 << 20

F32 = jnp.float32
BF16 = jnp.bfloat16


def _rms(x, g):
    return x * lax.rsqrt(jnp.mean(x * x, axis=-1, keepdims=True) + EPS) * g


def _sigmoid(x):
    return 0.5 * jnp.tanh(0.5 * x) + 0.5


def _resident(shape):
    nd = len(shape)
    return pl.BlockSpec(shape, lambda *_: (0,) * nd, pipeline_mode=pl.Buffered(1))


_HBM = pl.BlockSpec(memory_space=pl.ANY)


def _chunk_rows(k, n):
    best = BF16_ROWS
    for r in range(BF16_ROWS, k + 1, BF16_ROWS):
        if k % r == 0 and r * n * 4 <= STAGE_BYTES:
            best = r
    return best


def _stage_shape(k, n):
    return (STAGE_SLOTS, _chunk_rows(k, n), n)


def _stream_weights_to_bf16(jobs, sems):
    chunks = []
    for src, dst, stage in jobs:
        rows = stage.shape[1]
        for c in range(src.shape[0] // rows):
            chunks.append((src, dst, stage, c * rows, rows))

    def copy(j):
        src, _, stage, r0, rows = chunks[j]
        slot = j % STAGE_SLOTS
        return pltpu.make_async_copy(src.at[pl.ds(r0, rows), :], stage.at[slot], sems.at[slot])

    ahead = STAGE_SLOTS - 1
    for j in range(min(ahead, len(chunks))):
        copy(j).start()
    for j, (_, dst, stage, r0, rows) in enumerate(chunks):
        if j + ahead < len(chunks):
            copy(j + ahead).start()
        copy(j).wait()
        dst[r0:r0 + rows, :] = stage[j % STAGE_SLOTS].astype(BF16)


def _dot_nt(a, b_t):
    return lax.dot_general(a, b_t, (((1,), (1,)), ((), ())), preferred_element_type=F32)


def _ffn_rows(x, g_ref, wg_v, wu_v, wd_v, fg_ref, final_norm):
    h = _rms(x, g_ref[...]).astype(BF16)
    dff = wd_v.shape[0]
    cut = (dff // FFN_HIDDEN_PARTS) // MXU_DIM * MXU_DIM
    bounds = [i * cut for i in range(FFN_HIDDEN_PARTS)] + [dff]
    down = None
    for lo, hi in zip(bounds[:-1], bounds[1:]):
        gate = _dot_nt(h, wg_v[lo:hi, :])
        up = _dot_nt(h, wu_v[lo:hi, :])
        act = (gate * _sigmoid(gate) * up).astype(BF16)
        part = jnp.dot(act, wd_v[lo:hi, :], preferred_element_type=F32)
        down = part if down is None else down + part
    y = x + 0.5 * down
    return _rms(y, fg_ref[...]) if final_norm else y


def _ffn_kernel(xp_ref, xs_ref, g_ref, fg_ref, wg_hbm, wu_hbm, wd_hbm,
                op_ref, os_ref,
                wg_v, wu_v, wd_v, stage, sems, *, n_tiles, final_norm):
    step = pl.program_id(0)

    @pl.when(step == 0)
    def _():
        _stream_weights_to_bf16([(wg_hbm, wg_v, stage), (wu_hbm, wu_v, stage), (wd_hbm, wd_v, stage)], sems)

    @pl.when(step < n_tiles)
    def _():
        op_ref[...] = _ffn_rows(xp_ref[...], g_ref, wg_v, wu_v, wd_v, fg_ref, final_norm)

    @pl.when(step == n_tiles)
    def _():
        os_ref[...] = _ffn_rows(xs_ref[...], g_ref, wg_v, wu_v, wd_v, fg_ref, final_norm)


def _ffn(xp2d, xs2d, g, fg, wg_t, wu_t, wd, *, final_norm):
    m, d = xp2d.shape
    ms = xs2d.shape[0]
    dff = wd.shape[0]
    tm = FFN_TILE
    n_tiles = m // tm
    prompt_rows = pl.BlockSpec((tm, d), lambda i: (jnp.minimum(i, n_tiles - 1), 0))
    sample_rows = pl.BlockSpec((ms, d), lambda i: (0, 0))
    stage = _stage_shape(dff, d)
    vmem = (2 * 3 * d * dff
            + 4 * stage[0] * stage[1] * stage[2]
            + 4 * 4 * (tm + ms) * d
            + tm * (2 * 4 + 2) * dff // FFN_HIDDEN_PARTS
            + (2 + 4 * 4) * tm * d)
    return pl.pallas_call(
        functools.partial(_ffn_kernel, n_tiles=n_tiles, final_norm=final_norm),
        grid=(n_tiles + 1,),
        in_specs=[prompt_rows, sample_rows, _resident(g.shape), _resident(fg.shape), _HBM, _HBM, _HBM],
        out_specs=[prompt_rows, sample_rows],
        out_shape=[jax.ShapeDtypeStruct((m, d), F32), jax.ShapeDtypeStruct((ms, d), F32)],
        scratch_shapes=[pltpu.VMEM((dff, d), BF16)] * 3
                       + [pltpu.VMEM(stage, F32), pltpu.SemaphoreType.DMA((STAGE_SLOTS,))],
        compiler_params=pltpu.CompilerParams(dimension_semantics=("arbitrary",), vmem_limit_bytes=vmem),
    )(xp2d, xs2d, g, fg, wg_t, wu_t, wd)


def _in_proj(h, win_v, k):
    return jnp.dot(h, win_v[:, k * D_MODEL:(k + 1) * D_MODEL], preferred_element_type=F32)


def _lru_coeffs(xc, wr_ref, br_ref, wi_ref, bi_ref, lam_ref):
    xcb = xc.astype(BF16)
    r_parts, i_parts = [], []
    for j in range(GATE_BLOCKS):
        blk = xcb[:, j * MXU_DIM:(j + 1) * MXU_DIM]
        r_parts.append(jnp.dot(blk, wr_ref[j], preferred_element_type=F32))
        i_parts.append(jnp.dot(blk, wi_ref[j], preferred_element_type=F32))
    lam = lam_ref[...]
    half_rate = (-0.5 * LRU_C) * (jnp.maximum(-lam, 0.0) + jnp.log1p(jnp.exp(-jnp.abs(lam))))
    t_r = jnp.tanh(0.5 * jnp.concatenate(r_parts, axis=-1) + 0.5 * br_ref[...])
    t_i = jnp.tanh(0.5 * jnp.concatenate(i_parts, axis=-1) + 0.5 * bi_ref[...])
    log_a = t_r * half_rate + half_rate
    a = jnp.exp(log_a)
    th = jnp.tanh(log_a)
    num = -2.0 * th
    root = jnp.where(num == 0.0, 0.0, num * lax.rsqrt(num)) * lax.rsqrt(1.0 - th)
    half_xc = 0.5 * xc
    b = root * (t_i * half_xc + half_xc)
    return a, b


def _merge(x, ya_b16, yb, g_a, g_b, pa_v, pb_v, wo_v):
    m = (_sigmoid(g_a) * jnp.dot(ya_b16, pa_v[...], preferred_element_type=F32)
         + _sigmoid(g_b) * jnp.dot(yb.astype(BF16), pb_v[...], preferred_element_type=F32))
    return x + jnp.dot(m.astype(BF16), wo_v[...], preferred_element_type=F32)


def _mixer_prompt_tile(x_ref, p, win_v, pa_v, pb_v, wo_v, xo_ref, hlast_ref, convnew_ref,
                       xb_scr, a_scr, b_scr, ya_scr, h_scr, first_tile):
    tm = x_ref.shape[0]

    if first_tile is not None:
        @pl.when(first_tile)
        def _():
            xb_scr[0:HIST, :] = jnp.zeros((HIST, D_MODEL), F32)
            h_scr[...] = jnp.zeros((1, D_MODEL), F32)

    x = x_ref[...]
    h = _rms(x, p["mix_norm"][...]).astype(BF16)

    u = jax.nn.gelu(_in_proj(h, win_v, 0))
    v = _rms(jax.nn.gelu(_in_proj(h, win_v, 1)), p["v_norm"][...]).astype(BF16)
    causal = (lax.broadcasted_iota(jnp.int32, (CHUNK, CHUNK), 0)
              >= lax.broadcasted_iota(jnp.int32, (CHUNK, CHUNK), 1))
    for g in range(A_GROUPS):
        w_g = jnp.where(causal, p["spatial_w"][g], 0.0).astype(BF16)
        bias_g = p["spatial_b_t"][:, g:g + 1]
        cols = slice(g * A_GROUP_DIM, (g + 1) * A_GROUP_DIM)
        for c in range(tm // CHUNK):
            rows = slice(c * CHUNK, (c + 1) * CHUNK)
            s = jnp.dot(w_g, v[rows, cols], preferred_element_type=F32) + bias_g
            ya_scr[rows, cols] = (u[rows, cols] * s).astype(BF16)

    xb = _in_proj(h, win_v, 2)
    xb_scr[HIST:HIST + tm, :] = xb
    cw = p["conv_w"]
    xc = p["conv_b"][...] + cw[CONV_WIDTH - 1:CONV_WIDTH, :] * xb
    for k in range(CONV_WIDTH - 1):
        back = CONV_WIDTH - 1 - k
        xc = xc + cw[k:k + 1, :] * xb_scr[HIST - back:HIST - back + tm, :]
    convnew_ref[...] = xb_scr[HIST + tm - (CONV_WIDTH - 1):HIST + tm, :]
    xb_scr[0:HIST, :] = xb_scr[tm:tm + HIST, :]

    a, b = _lru_coeffs(xc, p["w_r"], p["b_r"], p["w_i"], p["b_i"], p["lam"])
    a_scr[...] = a
    b_scr[...] = b

    first_row = lax.broadcasted_iota(jnp.int32, (SUBLANES, D_MODEL), 0) == 0
    carry = h_scr[...]
    for r0 in range(0, tm, SUBLANES):
        ag = a_scr[r0:r0 + SUBLANES, :]
        hg = jnp.where(first_row, ag, 0.0) * carry + b_scr[r0:r0 + SUBLANES, :]
        span = jnp.where(first_row, 0.0, ag)
        d = 1
        while d < SUBLANES:
            hg = span * pltpu.roll(hg, d, 0) + hg
            if 2 * d < SUBLANES:
                span = span * pltpu.roll(span, d, 0)
            d *= 2
        b_scr[r0:r0 + SUBLANES, :] = hg
        carry = hg[SUBLANES - 1:SUBLANES, :]
    h_scr[...] = carry
    hlast_ref[...] = carry

    yb = b_scr[...] * jax.nn.gelu(_in_proj(h, win_v, 3))
    xo_ref[...] = _merge(x, ya_scr[...], yb, _in_proj(h, win_v, 4), _in_proj(h, win_v, 5),
                         pa_v, pb_v, wo_v)


def _mixer_sample_rows(x_ref, h0_ref, cprev_ref, p, win_v, pa_v, pb_v, wo_v,
                       xo_ref, hnew_ref, xbnew_ref, v_ref):
    x = x_ref[...]
    h = _rms(x, p["mix_norm"][...]).astype(BF16)

    u = jax.nn.gelu(_in_proj(h, win_v, 0))
    v = _rms(jax.nn.gelu(_in_proj(h, win_v, 1)), p["v_norm"][...])
    v_ref[...] = v
    ya = u * (p["spatial_w0"][...] * v + p["spatial_b0"][...])

    xb = _in_proj(h, win_v, 2)
    xbnew_ref[...] = xb
    cw = p["conv_w"]
    xc = p["conv_b"][...] + cw[CONV_WIDTH - 1:CONV_WIDTH, :] * xb
    for k in range(CONV_WIDTH - 1):
        xc = xc + cw[k:k + 1, :] * cprev_ref[k]

    a, b = _lru_coeffs(xc, p["w_r"], p["b_r"], p["w_i"], p["b_i"], p["lam"])
    h_new = a * h0_ref[...] + b
    hnew_ref[...] = h_new

    yb = h_new * jax.nn.gelu(_in_proj(h, win_v, 3))
    xo_ref[...] = _merge(x, ya.astype(BF16), yb, _in_proj(h, win_v, 4), _in_proj(h, win_v, 5),
                         pa_v, pb_v, wo_v)


_SMALL_PARAMS = ("mix_norm", "v_norm", "spatial_w", "spatial_b_t", "spatial_w0", "spatial_b0",
                 "conv_w", "conv_b", "w_r", "b_r", "w_i", "b_i", "lam")


def _mixer_kernel(*refs, n_tiles, tiles_per_seq):
    n_small = len(_SMALL_PARAMS)
    xp_ref, xs_ref, h0_ref, cprev_ref = refs[:4]
    p = dict(zip(_SMALL_PARAMS, refs[4:4 + n_small]))
    win_hbm, pa_hbm, pb_hbm, wo_hbm = refs[4 + n_small:8 + n_small]
    xo_p, hlast_ref, convnew_ref, xo_s, hnew_ref, xbnew_ref, v_ref = refs[8 + n_small:15 + n_small]
    (win_v, pa_v, pb_v, wo_v, stage_in, stage_sq, sems,
     xb_scr, a_scr, b_scr, ya_scr, h_scr) = refs[15 + n_small:]
    step = pl.program_id(0)

    @pl.when(step == 0)
    def _():
        _stream_weights_to_bf16([(win_hbm, win_v, stage_in), (pa_hbm, pa_v, stage_sq),
                                 (pb_hbm, pb_v, stage_sq), (wo_hbm, wo_v, stage_sq)], sems)

    @pl.when(step < n_tiles)
    def _():
        for r0 in range(0, xp_ref.shape[0], MIX_SUB):
            rows = pl.ds(r0, MIX_SUB)
            _mixer_prompt_tile(xp_ref.at[rows], p, win_v, pa_v, pb_v, wo_v, xo_p.at[rows],
                               hlast_ref, convnew_ref, xb_scr, a_scr, b_scr, ya_scr, h_scr,
                               step % tiles_per_seq == 0 if r0 == 0 else None)

    @pl.when(step == n_tiles)
    def _():
        _mixer_sample_rows(xs_ref, h0_ref, cprev_ref, p, win_v, pa_v, pb_v, wo_v,
                           xo_s, hnew_ref, xbnew_ref, v_ref)


def _mixer(xp, xs, h0, cprev, small, win, pa, pb, wo):
    bsz, seq, d = xp.shape
    ms = xs.shape[0]
    tm = MIX_TILE
    tiles_per_seq = seq // tm
    n_tiles = bsz * tiles_per_seq
    win_cols = win.shape[1]

    def tile_of(i):
        j = jnp.minimum(i, n_tiles - 1)
        return j // tiles_per_seq, j % tiles_per_seq

    prompt_rows = pl.BlockSpec((None, tm, d), lambda i: (*tile_of(i), 0))
    per_seq = lambda rows: pl.BlockSpec((None, rows, d), lambda i: (tile_of(i)[0], 0, 0))
    sample_rows = pl.BlockSpec((ms, d), lambda i: (0, 0))
    small_args = [small[k] for k in _SMALL_PARAMS]
    stage_in, stage_sq = _stage_shape(d, win_cols), _stage_shape(d, d)
    vmem = (2 * d * (win_cols + 3 * d)
            + 4 * (stage_in[0] * stage_in[1] * stage_in[2] + stage_sq[0] * stage_sq[1] * stage_sq[2])
            + sum(a.size * a.dtype.itemsize for a in small_args) * 2
            + 4 * 4 * tm * d + 4 * 16 * ms * d
            + (4 * 3 + 2) * tm * d + 4 * HIST * d
            + 4 * 4 * tm * d)
    return pl.pallas_call(
        functools.partial(_mixer_kernel, n_tiles=n_tiles, tiles_per_seq=tiles_per_seq),
        grid=(n_tiles + 1,),
        in_specs=[prompt_rows, sample_rows, sample_rows, _resident(cprev.shape)]
                 + [_resident(a.shape) for a in small_args] + [_HBM] * 4,
        out_specs=[prompt_rows, per_seq(1), per_seq(CONV_WIDTH - 1)] + [sample_rows] * 4,
        out_shape=[jax.ShapeDtypeStruct((bsz, seq, d), F32),
                   jax.ShapeDtypeStruct((bsz, 1, d), F32),
                   jax.ShapeDtypeStruct((bsz, CONV_WIDTH - 1, d), F32)]
                  + [jax.ShapeDtypeStruct((ms, d), F32)] * 4,
        scratch_shapes=[pltpu.VMEM((d, win_cols), BF16), pltpu.VMEM((d, d), BF16),
                        pltpu.VMEM((d, d), BF16), pltpu.VMEM((d, d), BF16),
                        pltpu.VMEM(stage_in, F32), pltpu.VMEM(stage_sq, F32),
                        pltpu.SemaphoreType.DMA((STAGE_SLOTS,)),
                        pltpu.VMEM((HIST + MIX_SUB, d), F32),
                        pltpu.VMEM((MIX_SUB, d), F32),
                        pltpu.VMEM((MIX_SUB, d), F32),
                        pltpu.VMEM((MIX_SUB, d), BF16),
                        pltpu.VMEM((1, d), F32)],
        compiler_params=pltpu.CompilerParams(dimension_semantics=("arbitrary",), vmem_limit_bytes=vmem),
    )(xp, xs, h0, cprev, *small_args, win, pa, pb, wo)


def _block_diag_tiles(w):
    w = w.reshape(GATE_BLOCKS, HEADS_PER_BLOCK, B_HEAD_DIM, B_HEAD_DIM)
    eye = jnp.eye(HEADS_PER_BLOCK, dtype=w.dtype)
    t = jnp.einsum('bhij,hk->bhikj', w, eye)
    return t.reshape(GATE_BLOCKS, MXU_DIM, MXU_DIM)


def kernel(x_prompt, x_sample, state_lru_h, state_conv, ffn1_norm, ffn1_w_gate, ffn1_w_up, ffn1_w_down, mix_norm, w_in, gmlp_v_norm, spatial_w, spatial_b, conv_w, conv_b, lru_w_r, lru_b_r, lru_w_i, lru_b_i, lru_lambda, proj_a, proj_b, w_out, ffn2_norm, ffn2_w_gate, ffn2_w_up, ffn2_w_down, final_norm):
    depth = ffn1_norm.shape[0]
    bsz, seq, d = x_prompt.shape
    nsamp = x_sample.shape[0]
    row = lambda p: p.reshape(1, -1)
    fin = row(final_norm)

    xp = x_prompt
    xs = x_sample.reshape(nsamp, d)
    h_p, c_p, h_s, c_s, v_s = [], [], [], [], []
    for l in range(depth):
        small = {
            "mix_norm": row(mix_norm[l]), "v_norm": row(gmlp_v_norm[l]),
            "spatial_w": spatial_w[l], "spatial_b_t": spatial_b[l].T,
            "spatial_w0": row(jnp.repeat(spatial_w[l][:, 0, 0], A_GROUP_DIM)),
            "spatial_b0": row(jnp.repeat(spatial_b[l][:, 0], A_GROUP_DIM)),
            "conv_w": conv_w[l], "conv_b": row(conv_b[l]),
            "w_r": _block_diag_tiles(lru_w_r[l]).astype(BF16), "b_r": row(lru_b_r[l]),
            "w_i": _block_diag_tiles(lru_w_i[l]).astype(BF16), "b_i": row(lru_b_i[l]),
            "lam": row(lru_lambda[l]),
        }
        last = l == depth - 1

        xp2d, xs = _ffn(xp.reshape(bsz * seq, d), xs, row(ffn1_norm[l]), fin,
                        ffn1_w_gate[l].T, ffn1_w_up[l].T, ffn1_w_down[l], final_norm=False)
        cprev = jnp.swapaxes(state_conv[l], 0, 1)
        xp, hl, cn, xs, hn, xb_new, v_new = _mixer(
            xp2d.reshape(bsz, seq, d), xs, state_lru_h[l], cprev, small,
            w_in[l], proj_a[l], proj_b[l], w_out[l])
        xp2d, xs = _ffn(xp.reshape(bsz * seq, d), xs, row(ffn2_norm[l]), fin,
                        ffn2_w_gate[l].T, ffn2_w_up[l].T, ffn2_w_down[l], final_norm=last)
        xp = xp2d.reshape(bsz, seq, d)

        h_p.append(hl.reshape(bsz, d))
        c_p.append(cn)
        h_s.append(hn)
        c_s.append(jnp.concatenate([state_conv[l][:, 1:], xb_new[:, None, :]], axis=1))
        v_s.append(v_new.reshape(nsamp, 1, d))

    return (xp, xs.reshape(nsamp, 1, d), jnp.stack(h_p), jnp.stack(c_p),
            jnp.stack(h_s), jnp.stack(c_s), jnp.stack(v_s))
```

```python
import functools

import jax
import jax.numpy as jnp
from jax import lax
from jax.experimental import pallas as pl
from jax.experimental.pallas import tpu as pltpu

D_MODEL = 1024
CHUNK = 128
A_GROUPS = 8
A_GROUP_DIM = D_MODEL // A_GROUPS
B_HEADS = 16
B_HEAD_DIM = D_MODEL // B_HEADS
CONV_WIDTH = 4
LRU_C = 8.0
EPS = 1e-6

SUBLANES = 8
BF16_ROWS = 16
MXU_DIM = 256
GATE_BLOCKS = D_MODEL // MXU_DIM
HEADS_PER_BLOCK = MXU_DIM // B_HEAD_DIM

FFN_TILE = 1024
FFN_HIDDEN_PARTS = 2
MIX_TILE = 256
HIST = SUBLANES
STAGE_SLOTS = 2
STAGE_BYTES = 3 << 20

F32 = jnp.float32
BF16 = jnp.bfloat16


def _rms(x, g):
    return x * lax.rsqrt(jnp.mean(x * x, axis=-1, keepdims=True) + EPS) * g


def _sigmoid(x):
    return 0.5 * jnp.tanh(0.5 * x) + 0.5


def _resident(shape):
    nd = len(shape)
    return pl.BlockSpec(shape, lambda *_: (0,) * nd, pipeline_mode=pl.Buffered(1))


_HBM = pl.BlockSpec(memory_space=pl.ANY)


def _chunk_rows(k, n):
    best = BF16_ROWS
    for r in range(BF16_ROWS, k + 1, BF16_ROWS):
        if k % r == 0 and r * n * 4 <= STAGE_BYTES:
            best = r
    return best


def _stage_shape(k, n):
    return (STAGE_SLOTS, _chunk_rows(k, n), n)


def _stream_weights_to_bf16(jobs, sems):
    chunks = []
    for src, dst, stage in jobs:
        rows = stage.shape[1]
        for c in range(src.shape[0] // rows):
            chunks.append((src, dst, stage, c * rows, rows))

    def copy(j):
        src, _, stage, r0, rows = chunks[j]
        slot = j % STAGE_SLOTS
        return pltpu.make_async_copy(src.at[pl.ds(r0, rows), :], stage.at[slot], sems.at[slot])

    ahead = STAGE_SLOTS - 1
    for j in range(min(ahead, len(chunks))):
        copy(j).start()
    for j, (_, dst, stage, r0, rows) in enumerate(chunks):
        if j + ahead < len(chunks):
            copy(j + ahead).start()
        copy(j).wait()
        dst[r0:r0 + rows, :] = stage[j % STAGE_SLOTS].astype(BF16)


def _dot_nt(a, b_t):
    return lax.dot_general(a, b_t, (((1,), (1,)), ((), ())), preferred_element_type=F32)


def _ffn_rows(x, g_ref, wg_v, wu_v, wd_v, fg_ref, final_norm):
    h = _rms(x, g_ref[...]).astype(BF16)
    dff = wd_v.shape[0]
    cut = (dff // FFN_HIDDEN_PARTS) // MXU_DIM * MXU_DIM
    bounds = [i * cut for i in range(FFN_HIDDEN_PARTS)] + [dff]
    down = None
    for lo, hi in zip(bounds[:-1], bounds[1:]):
        gate = _dot_nt(h, wg_v[lo:hi, :])
        up = _dot_nt(h, wu_v[lo:hi, :])
        act = (gate * _sigmoid(gate) * up).astype(BF16)
        part = jnp.dot(act, wd_v[lo:hi, :], preferred_element_type=F32)
        down = part if down is None else down + part
    y = x + 0.5 * down
    return _rms(y, fg_ref[...]) if final_norm else y


def _ffn_kernel(xp_ref, xs_ref, g_ref, fg_ref, wg_hbm, wu_hbm, wd_hbm,
                op_ref, os_ref,
                wg_v, wu_v, wd_v, stage, sems, *, n_tiles, final_norm):
    step = pl.program_id(0)

    @pl.when(step == 0)
    def _():
        _stream_weights_to_bf16([(wg_hbm, wg_v, stage), (wu_hbm, wu_v, stage), (wd_hbm, wd_v, stage)], sems)

    op_ref[...] = _ffn_rows(xp_ref[...], g_ref, wg_v, wu_v, wd_v, fg_ref, final_norm)

    @pl.when(step == n_tiles - 1)
    def _():
        os_ref[...] = _ffn_rows(xs_ref[...], g_ref, wg_v, wu_v, wd_v, fg_ref, final_norm)


def _ffn(xp2d, xs2d, g, fg, wg_t, wu_t, wd, *, final_norm):
    m, d = xp2d.shape
    ms = xs2d.shape[0]
    dff = wd.shape[0]
    tm = FFN_TILE
    n_tiles = m // tm
    prompt_rows = pl.BlockSpec((tm, d), lambda i: (i, 0))
    sample_rows = pl.BlockSpec((ms, d), lambda i: (0, 0))
    stage = _stage_shape(dff, d)
    vmem = (2 * 3 * d * dff
            + 4 * stage[0] * stage[1] * stage[2]
            + 4 * 4 * (tm + ms) * d
            + tm * (2 * 4 + 2) * dff // FFN_HIDDEN_PARTS
            + (2 + 4 * 4) * tm * d)
    return pl.pallas_call(
        functools.partial(_ffn_kernel, n_tiles=n_tiles, final_norm=final_norm),
        grid=(n_tiles,),
        in_specs=[prompt_rows, sample_rows, _resident(g.shape), _resident(fg.shape), _HBM, _HBM, _HBM],
        out_specs=[prompt_rows, sample_rows],
        out_shape=[jax.ShapeDtypeStruct((m, d), F32), jax.ShapeDtypeStruct((ms, d), F32)],
        scratch_shapes=[pltpu.VMEM((dff, d), BF16)] * 3
                       + [pltpu.VMEM(stage, F32), pltpu.SemaphoreType.DMA((STAGE_SLOTS,))],
        compiler_params=pltpu.CompilerParams(dimension_semantics=("arbitrary",), vmem_limit_bytes=vmem),
    )(xp2d, xs2d, g, fg, wg_t, wu_t, wd)


def _in_proj(h, win_v, k):
    return jnp.dot(h, win_v[:, k * D_MODEL:(k + 1) * D_MODEL], preferred_element_type=F32)


def _lru_coeffs(xc, wr_ref, br_ref, wi_ref, bi_ref, lam_ref):
    xcb = xc.astype(BF16)
    r_parts, i_parts = [], []
    for j in range(GATE_BLOCKS):
        blk = xcb[:, j * MXU_DIM:(j + 1) * MXU_DIM]
        r_parts.append(jnp.dot(blk, wr_ref[j], preferred_element_type=F32))
        i_parts.append(jnp.dot(blk, wi_ref[j], preferred_element_type=F32))
    lam = lam_ref[...]
    half_rate = (-0.5 * LRU_C) * (jnp.maximum(-lam, 0.0) + jnp.log1p(jnp.exp(-jnp.abs(lam))))
    t_r = jnp.tanh(0.5 * jnp.concatenate(r_parts, axis=-1) + 0.5 * br_ref[...])
    t_i = jnp.tanh(0.5 * jnp.concatenate(i_parts, axis=-1) + 0.5 * bi_ref[...])
    log_a = t_r * half_rate + half_rate
    a = jnp.exp(log_a)
    th = jnp.tanh(log_a)
    num = -2.0 * th
    root = jnp.where(num == 0.0, 0.0, num * lax.rsqrt(num)) * lax.rsqrt(1.0 - th)
    half_xc = 0.5 * xc
    b = root * (t_i * half_xc + half_xc)
    return a, b


def _merge(x, ya_b16, yb, g_a, g_b, pa_v, pb_v, wo_v):
    m = (_sigmoid(g_a) * jnp.dot(ya_b16, pa_v[...], preferred_element_type=F32)
         + _sigmoid(g_b) * jnp.dot(yb.astype(BF16), pb_v[...], preferred_element_type=F32))
    return x + jnp.dot(m.astype(BF16), wo_v[...], preferred_element_type=F32)


def _mixer_prompt_tile(x_ref, p, win_v, pa_v, pb_v, wo_v, xo_ref, hlast_ref, convnew_ref,
                       xb_scr, a_scr, b_scr, ya_scr, h_scr, first_tile):
    tm = x_ref.shape[0]

    @pl.when(first_tile)
    def _():
        xb_scr[0:HIST, :] = jnp.zeros((HIST, D_MODEL), F32)
        h_scr[...] = jnp.zeros((1, D_MODEL), F32)

    x = x_ref[...]
    h = _rms(x, p["mix_norm"][...]).astype(BF16)

    u = jax.nn.gelu(_in_proj(h, win_v, 0))
    v = _rms(jax.nn.gelu(_in_proj(h, win_v, 1)), p["v_norm"][...]).astype(BF16)
    causal = (lax.broadcasted_iota(jnp.int32, (CHUNK, CHUNK), 0)
              >= lax.broadcasted_iota(jnp.int32, (CHUNK, CHUNK), 1))
    for g in range(A_GROUPS):
        w_g = jnp.where(causal, p["spatial_w"][g], 0.0).astype(BF16)
        bias_g = p["spatial_b_t"][:, g:g + 1]
        cols = slice(g * A_GROUP_DIM, (g + 1) * A_GROUP_DIM)
        for c in range(tm // CHUNK):
            rows = slice(c * CHUNK, (c + 1) * CHUNK)
            s = jnp.dot(w_g, v[rows, cols], preferred_element_type=F32) + bias_g
            ya_scr[rows, cols] = (u[rows, cols] * s).astype(BF16)

    xb = _in_proj(h, win_v, 2)
    xb_scr[HIST:HIST + tm, :] = xb
    cw = p["conv_w"]
    xc = p["conv_b"][...] + cw[CONV_WIDTH - 1:CONV_WIDTH, :] * xb
    for k in range(CONV_WIDTH - 1):
        back = CONV_WIDTH - 1 - k
        xc = xc + cw[k:k + 1, :] * xb_scr[HIST - back:HIST - back + tm, :]
    convnew_ref[...] = xb_scr[HIST + tm - (CONV_WIDTH - 1):HIST + tm, :]
    xb_scr[0:HIST, :] = xb_scr[tm:tm + HIST, :]

    a, b = _lru_coeffs(xc, p["w_r"], p["b_r"], p["w_i"], p["b_i"], p["lam"])
    a_scr[...] = a
    b_scr[...] = b

    first_row = lax.broadcasted_iota(jnp.int32, (SUBLANES, D_MODEL), 0) == 0
    carry = h_scr[...]
    for r0 in range(0, tm, SUBLANES):
        ag = a_scr[r0:r0 + SUBLANES, :]
        hg = jnp.where(first_row, ag, 0.0) * carry + b_scr[r0:r0 + SUBLANES, :]
        span = jnp.where(first_row, 0.0, ag)
        d = 1
        while d < SUBLANES:
            hg = span * pltpu.roll(hg, d, 0) + hg
            if 2 * d < SUBLANES:
                span = span * pltpu.roll(span, d, 0)
            d *= 2
        b_scr[r0:r0 + SUBLANES, :] = hg
        carry = hg[SUBLANES - 1:SUBLANES, :]
    h_scr[...] = carry
    hlast_ref[...] = carry

    yb = b_scr[...] * jax.nn.gelu(_in_proj(h, win_v, 3))
    xo_ref[...] = _merge(x, ya_scr[...], yb, _in_proj(h, win_v, 4), _in_proj(h, win_v, 5),
                         pa_v, pb_v, wo_v)


def _mixer_sample_rows(x_ref, h0_ref, cprev_ref, p, win_v, pa_v, pb_v, wo_v,
                       xo_ref, hnew_ref, xbnew_ref, v_ref):
    x = x_ref[...]
    h = _rms(x, p["mix_norm"][...]).astype(BF16)

    u = jax.nn.gelu(_in_proj(h, win_v, 0))
    v = _rms(jax.nn.gelu(_in_proj(h, win_v, 1)), p["v_norm"][...])
    v_ref[...] = v
    ya = u * (p["spatial_w0"][...] * v + p["spatial_b0"][...])

    xb = _in_proj(h, win_v, 2)
    xbnew_ref[...] = xb
    cw = p["conv_w"]
    xc = p["conv_b"][...] + cw[CONV_WIDTH - 1:CONV_WIDTH, :] * xb
    for k in range(CONV_WIDTH - 1):
        xc = xc + cw[k:k + 1, :] * cprev_ref[k]

    a, b = _lru_coeffs(xc, p["w_r"], p["b_r"], p["w_i"], p["b_i"], p["lam"])
    h_new = a * h0_ref[...] + b
    hnew_ref[...] = h_new

    yb = h_new * jax.nn.gelu(_in_proj(h, win_v, 3))
    xo_ref[...] = _merge(x, ya.astype(BF16), yb, _in_proj(h, win_v, 4), _in_proj(h, win_v, 5),
                         pa_v, pb_v, wo_v)


_SMALL_PARAMS = ("mix_norm", "v_norm", "spatial_w", "spatial_b_t", "spatial_w0", "spatial_b0",
                 "conv_w", "conv_b", "w_r", "b_r", "w_i", "b_i", "lam")


def _mixer_kernel(*refs, n_tiles, tiles_per_seq):
    n_small = len(_SMALL_PARAMS)
    xp_ref, xs_ref, h0_ref, cprev_ref = refs[:4]
    p = dict(zip(_SMALL_PARAMS, refs[4:4 + n_small]))
    win_hbm, pa_hbm, pb_hbm, wo_hbm = refs[4 + n_small:8 + n_small]
    xo_p, hlast_ref, convnew_ref, xo_s, hnew_ref, xbnew_ref, v_ref = refs[8 + n_small:15 + n_small]
    (win_v, pa_v, pb_v, wo_v, stage_in, stage_sq, sems,
     xb_scr, a_scr, b_scr, ya_scr, h_scr) = refs[15 + n_small:]
    step = pl.program_id(0)

    @pl.when(step == 0)
    def _():
        _stream_weights_to_bf16([(win_hbm, win_v, stage_in), (pa_hbm, pa_v, stage_sq),
                                 (pb_hbm, pb_v, stage_sq), (wo_hbm, wo_v, stage_sq)], sems)

    _mixer_prompt_tile(xp_ref, p, win_v, pa_v, pb_v, wo_v, xo_p, hlast_ref, convnew_ref,
                       xb_scr, a_scr, b_scr, ya_scr, h_scr, step % tiles_per_seq == 0)

    @pl.when(step == n_tiles - 1)
    def _():
        _mixer_sample_rows(xs_ref, h0_ref, cprev_ref, p, win_v, pa_v, pb_v, wo_v,
                           xo_s, hnew_ref, xbnew_ref, v_ref)


def _mixer(xp, xs, h0, cprev, small, win, pa, pb, wo):
    bsz, seq, d = xp.shape
    ms = xs.shape[0]
    tm = MIX_TILE
    tiles_per_seq = seq // tm
    n_tiles = bsz * tiles_per_seq
    win_cols = win.shape[1]

    def tile_of(i):
        return i // tiles_per_seq, i % tiles_per_seq

    prompt_rows = pl.BlockSpec((None, tm, d), lambda i: (*tile_of(i), 0))
    per_seq = lambda rows: pl.BlockSpec((None, rows, d), lambda i: (tile_of(i)[0], 0, 0))
    sample_rows = pl.BlockSpec((ms, d), lambda i: (0, 0))
    small_args = [small[k] for k in _SMALL_PARAMS]
    stage_in, stage_sq = _stage_shape(d, win_cols), _stage_shape(d, d)
    vmem = (2 * d * (win_cols + 3 * d)
            + 4 * (stage_in[0] * stage_in[1] * stage_in[2] + stage_sq[0] * stage_sq[1] * stage_sq[2])
            + sum(a.size * a.dtype.itemsize for a in small_args) * 2
            + 4 * 4 * tm * d + 4 * 16 * ms * d
            + (4 * 3 + 2) * tm * d + 4 * HIST * d
            + 4 * 4 * tm * d)
    return pl.pallas_call(
        functools.partial(_mixer_kernel, n_tiles=n_tiles, tiles_per_seq=tiles_per_seq),
        grid=(n_tiles,),
        in_specs=[prompt_rows, sample_rows, sample_rows, _resident(cprev.shape)]
                 + [_resident(a.shape) for a in small_args] + [_HBM] * 4,
        out_specs=[prompt_rows, per_seq(1), per_seq(CONV_WIDTH - 1)] + [sample_rows] * 4,
        out_shape=[jax.ShapeDtypeStruct((bsz, seq, d), F32),
                   jax.ShapeDtypeStruct((bsz, 1, d), F32),
                   jax.ShapeDtypeStruct((bsz, CONV_WIDTH - 1, d), F32)]
                  + [jax.ShapeDtypeStruct((ms, d), F32)] * 4,
        scratch_shapes=[pltpu.VMEM((d, win_cols), BF16), pltpu.VMEM((d, d), BF16),
                        pltpu.VMEM((d, d), BF16), pltpu.VMEM((d, d), BF16),
                        pltpu.VMEM(stage_in, F32), pltpu.VMEM(stage_sq, F32),
                        pltpu.SemaphoreType.DMA((STAGE_SLOTS,)),
                        pltpu.VMEM((HIST + tm, d), F32),
                        pltpu.VMEM((tm, d), F32),
                        pltpu.VMEM((tm, d), F32),
                        pltpu.VMEM((tm, d), BF16),
                        pltpu.VMEM((1, d), F32)],
        compiler_params=pltpu.CompilerParams(dimension_semantics=("arbitrary",), vmem_limit_bytes=vmem),
    )(xp, xs, h0, cprev, *small_args, win, pa, pb, wo)


def _block_diag_tiles(w):
    w = w.reshape(GATE_BLOCKS, HEADS_PER_BLOCK, B_HEAD_DIM, B_HEAD_DIM)
    eye = jnp.eye(HEADS_PER_BLOCK, dtype=w.dtype)
    t = jnp.einsum('bhij,hk->bhikj', w, eye)
    return t.reshape(GATE_BLOCKS, MXU_DIM, MXU_DIM)


def kernel(x_prompt, x_sample, state_lru_h, state_conv, ffn1_norm, ffn1_w_gate, ffn1_w_up, ffn1_w_down, mix_norm, w_in, gmlp_v_norm, spatial_w, spatial_b, conv_w, conv_b, lru_w_r, lru_b_r, lru_w_i, lru_b_i, lru_lambda, proj_a, proj_b, w_out, ffn2_norm, ffn2_w_gate, ffn2_w_up, ffn2_w_down, final_norm):
    depth = ffn1_norm.shape[0]
    bsz, seq, d = x_prompt.shape
    nsamp = x_sample.shape[0]
    row = lambda p: p.reshape(1, -1)
    fin = row(final_norm)

    xp = x_prompt
    xs = x_sample.reshape(nsamp, d)
    h_p, c_p, h_s, c_s, v_s = [], [], [], [], []
    for l in range(depth):
        small = {
            "mix_norm": row(mix_norm[l]), "v_norm": row(gmlp_v_norm[l]),
            "spatial_w": spatial_w[l], "spatial_b_t": spatial_b[l].T,
            "spatial_w0": row(jnp.repeat(spatial_w[l][:, 0, 0], A_GROUP_DIM)),
            "spatial_b0": row(jnp.repeat(spatial_b[l][:, 0], A_GROUP_DIM)),
            "conv_w": conv_w[l], "conv_b": row(conv_b[l]),
            "w_r": _block_diag_tiles(lru_w_r[l]).astype(BF16), "b_r": row(lru_b_r[l]),
            "w_i": _block_diag_tiles(lru_w_i[l]).astype(BF16), "b_i": row(lru_b_i[l]),
            "lam": row(lru_lambda[l]),
        }
        last = l == depth - 1

        xp2d, xs = _ffn(xp.reshape(bsz * seq, d), xs, row(ffn1_norm[l]), fin,
                        ffn1_w_gate[l].T, ffn1_w_up[l].T, ffn1_w_down[l], final_norm=False)
        cprev = jnp.swapaxes(state_conv[l], 0, 1)
        xp, hl, cn, xs, hn, xb_new, v_new = _mixer(
            xp2d.reshape(bsz, seq, d), xs, state_lru_h[l], cprev, small,
            w_in[l], proj_a[l], proj_b[l], w_out[l])
        xp2d, xs = _ffn(xp.reshape(bsz * seq, d), xs, row(ffn2_norm[l]), fin,
                        ffn2_w_gate[l].T, ffn2_w_up[l].T, ffn2_w_down[l], final_norm=last)
        xp = xp2d.reshape(bsz, seq, d)

        h_p.append(hl.reshape(bsz, d))
        c_p.append(cn)
        h_s.append(hn)
        c_s.append(jnp.concatenate([state_conv[l][:, 1:], xb_new[:, None, :]], axis=1))
        v_s.append(v_new.reshape(nsamp, 1, d))

    return (xp, xs.reshape(nsamp, 1, d), jnp.stack(h_p), jnp.stack(c_p),
            jnp.stack(h_s), jnp.stack(c_s), jnp.stack(v_s))
```

```python
import functools

import jax
import jax.numpy as jnp
from jax import lax
from jax.experimental import pallas as pl
from jax.experimental.pallas import tpu as pltpu

D_MODEL = 1024
CHUNK = 128
A_GROUPS = 8
A_GROUP_DIM = D_MODEL // A_GROUPS
B_HEADS = 16
B_HEAD_DIM = D_MODEL // B_HEADS
CONV_WIDTH = 4
LRU_C = 8.0
EPS = 1e-6

SUBLANES = 8
BF16_ROWS = 16
MXU_DIM = 256
GATE_BLOCKS = D_MODEL // MXU_DIM
HEADS_PER_BLOCK = MXU_DIM // B_HEAD_DIM

FFN_TILE = 1024
FFN_HIDDEN_PARTS = 2
MIX_TILE = 256
HIST = SUBLANES
STAGE_SLOTS = 3
STAGE_BYTES = 3 << 20

F32 = jnp.float32
BF16 = jnp.bfloat16


def _rms(x, g):
    return x * lax.rsqrt(jnp.mean(x * x, axis=-1, keepdims=True) + EPS) * g


def _sigmoid(x):
    return 0.5 * jnp.tanh(0.5 * x) + 0.5


def _resident(shape):
    nd = len(shape)
    return pl.BlockSpec(shape, lambda *_: (0,) * nd, pipeline_mode=pl.Buffered(1))


_HBM = pl.BlockSpec(memory_space=pl.ANY)


def _chunk_rows(k, n):
    best = BF16_ROWS
    for r in range(BF16_ROWS, k + 1, BF16_ROWS):
        if k % r == 0 and r * n * 4 <= STAGE_BYTES:
            best = r
    return best


def _stage_shape(k, n):
    return (STAGE_SLOTS, _chunk_rows(k, n), n)


def _stream_weights_to_bf16(jobs, sems):
    chunks = []
    for src, dst, stage in jobs:
        rows = stage.shape[1]
        for c in range(src.shape[0] // rows):
            chunks.append((src, dst, stage, c * rows, rows))

    def copy(j):
        src, _, stage, r0, rows = chunks[j]
        slot = j % STAGE_SLOTS
        return pltpu.make_async_copy(src.at[pl.ds(r0, rows), :], stage.at[slot], sems.at[slot])

    ahead = STAGE_SLOTS - 1
    for j in range(min(ahead, len(chunks))):
        copy(j).start(priority=j % 2)
    for j, (_, dst, stage, r0, rows) in enumerate(chunks):
        if j + ahead < len(chunks):
            copy(j + ahead).start(priority=(j + ahead) % 2)
        copy(j).wait()
        dst[r0:r0 + rows, :] = stage[j % STAGE_SLOTS].astype(BF16)


def _dot_nt(a, b_t):
    return lax.dot_general(a, b_t, (((1,), (1,)), ((), ())), preferred_element_type=F32)


def _ffn_rows(x, g_ref, wg_v, wu_v, wd_v, fg_ref, final_norm):
    h = _rms(x, g_ref[...]).astype(BF16)
    dff = wd_v.shape[0]
    cut = (dff // FFN_HIDDEN_PARTS) // MXU_DIM * MXU_DIM
    bounds = [i * cut for i in range(FFN_HIDDEN_PARTS)] + [dff]
    down = None
    for lo, hi in zip(bounds[:-1], bounds[1:]):
        gate = _dot_nt(h, wg_v[lo:hi, :])
        up = _dot_nt(h, wu_v[lo:hi, :])
        act = (gate * _sigmoid(gate) * up).astype(BF16)
        part = jnp.dot(act, wd_v[lo:hi, :], preferred_element_type=F32)
        down = part if down is None else down + part
    y = x + 0.5 * down
    return _rms(y, fg_ref[...]) if final_norm else y


def _ffn_kernel(xp_ref, xs_ref, g_ref, fg_ref, wg_hbm, wu_hbm, wd_hbm,
                op_ref, os_ref,
                wg_v, wu_v, wd_v, stage, sems, *, n_tiles, final_norm):
    step = pl.program_id(0)

    @pl.when(step == 0)
    def _():
        _stream_weights_to_bf16([(wg_hbm, wg_v, stage), (wu_hbm, wu_v, stage), (wd_hbm, wd_v, stage)], sems)

    @pl.when(step < n_tiles)
    def _():
        op_ref[...] = _ffn_rows(xp_ref[...], g_ref, wg_v, wu_v, wd_v, fg_ref, final_norm)

    @pl.when(step == n_tiles)
    def _():
        os_ref[...] = _ffn_rows(xs_ref[...], g_ref, wg_v, wu_v, wd_v, fg_ref, final_norm)


def _ffn(xp2d, xs2d, g, fg, wg_t, wu_t, wd, *, final_norm):
    m, d = xp2d.shape
    ms = xs2d.shape[0]
    dff = wd.shape[0]
    tm = FFN_TILE
    n_tiles = m // tm
    prompt_rows = pl.BlockSpec((tm, d), lambda i: (jnp.minimum(i, n_tiles - 1), 0))
    sample_rows = pl.BlockSpec((ms, d), lambda i: (0, 0))
    stage = _stage_shape(dff, d)
    vmem = (2 * 3 * d * dff
            + 4 * stage[0] * stage[1] * stage[2]
            + 4 * 4 * (tm + ms) * d
            + tm * (2 * 4 + 2) * dff // FFN_HIDDEN_PARTS
            + (2 + 4 * 4) * tm * d)
    return pl.pallas_call(
        functools.partial(_ffn_kernel, n_tiles=n_tiles, final_norm=final_norm),
        grid=(n_tiles + 1,),
        in_specs=[prompt_rows, sample_rows, _resident(g.shape), _resident(fg.shape), _HBM, _HBM, _HBM],
        out_specs=[prompt_rows, sample_rows],
        out_shape=[jax.ShapeDtypeStruct((m, d), F32), jax.ShapeDtypeStruct((ms, d), F32)],
        scratch_shapes=[pltpu.VMEM((dff, d), BF16)] * 3
                       + [pltpu.VMEM(stage, F32), pltpu.SemaphoreType.DMA((STAGE_SLOTS,))],
        compiler_params=pltpu.CompilerParams(dimension_semantics=("arbitrary",), vmem_limit_bytes=vmem),
    )(xp2d, xs2d, g, fg, wg_t, wu_t, wd)


def _in_proj(h, win_v, k):
    return jnp.dot(h, win_v[:, k * D_MODEL:(k + 1) * D_MODEL], preferred_element_type=F32)


def _lru_coeffs(xc, wr_ref, br_ref, wi_ref, bi_ref, lam_ref):
    xcb = xc.astype(BF16)
    r_parts, i_parts = [], []
    for j in range(GATE_BLOCKS):
        blk = xcb[:, j * MXU_DIM:(j + 1) * MXU_DIM]
        r_parts.append(jnp.dot(blk, wr_ref[j], preferred_element_type=F32))
        i_parts.append(jnp.dot(blk, wi_ref[j], preferred_element_type=F32))
    lam = lam_ref[...]
    half_rate = (-0.5 * LRU_C) * (jnp.maximum(-lam, 0.0) + jnp.log1p(jnp.exp(-jnp.abs(lam))))
    t_r = jnp.tanh(0.5 * jnp.concatenate(r_parts, axis=-1) + 0.5 * br_ref[...])
    t_i = jnp.tanh(0.5 * jnp.concatenate(i_parts, axis=-1) + 0.5 * bi_ref[...])
    log_a = t_r * half_rate + half_rate
    a = jnp.exp(log_a)
    th = jnp.tanh(log_a)
    num = -2.0 * th
    root = jnp.where(num == 0.0, 0.0, num * lax.rsqrt(num)) * lax.rsqrt(1.0 - th)
    half_xc = 0.5 * xc
    b = root * (t_i * half_xc + half_xc)
    return a, b


def _merge(x, ya_b16, yb, g_a, g_b, pa_v, pb_v, wo_v):
    m = (_sigmoid(g_a) * jnp.dot(ya_b16, pa_v[...], preferred_element_type=F32)
         + _sigmoid(g_b) * jnp.dot(yb.astype(BF16), pb_v[...], preferred_element_type=F32))
    return x + jnp.dot(m.astype(BF16), wo_v[...], preferred_element_type=F32)


def _mixer_prompt_tile(x_ref, p, win_v, pa_v, pb_v, wo_v, xo_ref, hlast_ref, convnew_ref,
                       xb_scr, a_scr, b_scr, ya_scr, h_scr, first_tile):
    tm = x_ref.shape[0]

    @pl.when(first_tile)
    def _():
        xb_scr[0:HIST, :] = jnp.zeros((HIST, D_MODEL), F32)
        h_scr[...] = jnp.zeros((1, D_MODEL), F32)

    x = x_ref[...]
    h = _rms(x, p["mix_norm"][...]).astype(BF16)

    u = jax.nn.gelu(_in_proj(h, win_v, 0))
    v = _rms(jax.nn.gelu(_in_proj(h, win_v, 1)), p["v_norm"][...]).astype(BF16)
    causal = (lax.broadcasted_iota(jnp.int32, (CHUNK, CHUNK), 0)
              >= lax.broadcasted_iota(jnp.int32, (CHUNK, CHUNK), 1))
    for g in range(A_GROUPS):
        w_g = jnp.where(causal, p["spatial_w"][g], 0.0).astype(BF16)
        bias_g = p["spatial_b_t"][:, g:g + 1]
        cols = slice(g * A_GROUP_DIM, (g + 1) * A_GROUP_DIM)
        for c in range(tm // CHUNK):
            rows = slice(c * CHUNK, (c + 1) * CHUNK)
            s = jnp.dot(w_g, v[rows, cols], preferred_element_type=F32) + bias_g
            ya_scr[rows, cols] = (u[rows, cols] * s).astype(BF16)

    xb = _in_proj(h, win_v, 2)
    xb_scr[HIST:HIST + tm, :] = xb
    cw = p["conv_w"]
    xc = p["conv_b"][...] + cw[CONV_WIDTH - 1:CONV_WIDTH, :] * xb
    for k in range(CONV_WIDTH - 1):
        back = CONV_WIDTH - 1 - k
        xc = xc + cw[k:k + 1, :] * xb_scr[HIST - back:HIST - back + tm, :]
    convnew_ref[...] = xb_scr[HIST + tm - (CONV_WIDTH - 1):HIST + tm, :]
    xb_scr[0:HIST, :] = xb_scr[tm:tm + HIST, :]

    a, b = _lru_coeffs(xc, p["w_r"], p["b_r"], p["w_i"], p["b_i"], p["lam"])
    a_scr[...] = a
    b_scr[...] = b

    first_row = lax.broadcasted_iota(jnp.int32, (SUBLANES, D_MODEL), 0) == 0
    carry = h_scr[...]
    for r0 in range(0, tm, SUBLANES):
        ag = a_scr[r0:r0 + SUBLANES, :]
        hg = jnp.where(first_row, ag, 0.0) * carry + b_scr[r0:r0 + SUBLANES, :]
        span = jnp.where(first_row, 0.0, ag)
        d = 1
        while d < SUBLANES:
            hg = span * pltpu.roll(hg, d, 0) + hg
            if 2 * d < SUBLANES:
                span = span * pltpu.roll(span, d, 0)
            d *= 2
        b_scr[r0:r0 + SUBLANES, :] = hg
        carry = hg[SUBLANES - 1:SUBLANES, :]
    h_scr[...] = carry
    hlast_ref[...] = carry

    yb = b_scr[...] * jax.nn.gelu(_in_proj(h, win_v, 3))
    xo_ref[...] = _merge(x, ya_scr[...], yb, _in_proj(h, win_v, 4), _in_proj(h, win_v, 5),
                         pa_v, pb_v, wo_v)


def _mixer_sample_rows(x_ref, h0_ref, cprev_ref, p, win_v, pa_v, pb_v, wo_v,
                       xo_ref, hnew_ref, xbnew_ref, v_ref):
    x = x_ref[...]
    h = _rms(x, p["mix_norm"][...]).astype(BF16)

    u = jax.nn.gelu(_in_proj(h, win_v, 0))
    v = _rms(jax.nn.gelu(_in_proj(h, win_v, 1)), p["v_norm"][...])
    v_ref[...] = v
    ya = u * (p["spatial_w0"][...] * v + p["spatial_b0"][...])

    xb = _in_proj(h, win_v, 2)
    xbnew_ref[...] = xb
    cw = p["conv_w"]
    xc = p["conv_b"][...] + cw[CONV_WIDTH - 1:CONV_WIDTH, :] * xb
    for k in range(CONV_WIDTH - 1):
        xc = xc + cw[k:k + 1, :] * cprev_ref[k]

    a, b = _lru_coeffs(xc, p["w_r"], p["b_r"], p["w_i"], p["b_i"], p["lam"])
    h_new = a * h0_ref[...] + b
    hnew_ref[...] = h_new

    yb = h_new * jax.nn.gelu(_in_proj(h, win_v, 3))
    xo_ref[...] = _merge(x, ya.astype(BF16), yb, _in_proj(h, win_v, 4), _in_proj(h, win_v, 5),
                         pa_v, pb_v, wo_v)


_SMALL_PARAMS = ("mix_norm", "v_norm", "spatial_w", "spatial_b_t", "spatial_w0", "spatial_b0",
                 "conv_w", "conv_b", "w_r", "b_r", "w_i", "b_i", "lam")


def _mixer_kernel(*refs, n_tiles, tiles_per_seq):
    n_small = len(_SMALL_PARAMS)
    xp_ref, xs_ref, h0_ref, cprev_ref = refs[:4]
    p = dict(zip(_SMALL_PARAMS, refs[4:4 + n_small]))
    win_hbm, pa_hbm, pb_hbm, wo_hbm = refs[4 + n_small:8 + n_small]
    xo_p, hlast_ref, convnew_ref, xo_s, hnew_ref, xbnew_ref, v_ref = refs[8 + n_small:15 + n_small]
    (win_v, pa_v, pb_v, wo_v, stage_in, stage_sq, sems,
     xb_scr, a_scr, b_scr, ya_scr, h_scr) = refs[15 + n_small:]
    step = pl.program_id(0)

    @pl.when(step == 0)
    def _():
        _stream_weights_to_bf16([(win_hbm, win_v, stage_in), (pa_hbm, pa_v, stage_sq),
                                 (pb_hbm, pb_v, stage_sq), (wo_hbm, wo_v, stage_sq)], sems)

    @pl.when(step < n_tiles)
    def _():
        _mixer_prompt_tile(xp_ref, p, win_v, pa_v, pb_v, wo_v, xo_p, hlast_ref, convnew_ref,
                           xb_scr, a_scr, b_scr, ya_scr, h_scr, step % tiles_per_seq == 0)

    @pl.when(step == n_tiles)
    def _():
        _mixer_sample_rows(xs_ref, h0_ref, cprev_ref, p, win_v, pa_v, pb_v, wo_v,
                           xo_s, hnew_ref, xbnew_ref, v_ref)


def _mixer(xp, xs, h0, cprev, small, win, pa, pb, wo):
    bsz, seq, d = xp.shape
    ms = xs.shape[0]
    tm = MIX_TILE
    tiles_per_seq = seq // tm
    n_tiles = bsz * tiles_per_seq
    win_cols = win.shape[1]

    def tile_of(i):
        j = jnp.minimum(i, n_tiles - 1)
        return j // tiles_per_seq, j % tiles_per_seq

    prompt_rows = pl.BlockSpec((None, tm, d), lambda i: (*tile_of(i), 0))
    per_seq = lambda rows: pl.BlockSpec((None, rows, d), lambda i: (tile_of(i)[0], 0, 0))
    sample_rows = pl.BlockSpec((ms, d), lambda i: (0, 0))
    small_args = [small[k] for k in _SMALL_PARAMS]
    stage_in, stage_sq = _stage_shape(d, win_cols), _stage_shape(d, d)
    vmem = (2 * d * (win_cols + 3 * d)
            + 4 * (stage_in[0] * stage_in[1] * stage_in[2] + stage_sq[0] * stage_sq[1] * stage_sq[2])
            + sum(a.size * a.dtype.itemsize for a in small_args) * 2
            + 4 * 4 * tm * d + 4 * 16 * ms * d
            + (4 * 3 + 2) * tm * d + 4 * HIST * d
            + 4 * 4 * tm * d)
    return pl.pallas_call(
        functools.partial(_mixer_kernel, n_tiles=n_tiles, tiles_per_seq=tiles_per_seq),
        grid=(n_tiles + 1,),
        in_specs=[prompt_rows, sample_rows, sample_rows, _resident(cprev.shape)]
                 + [_resident(a.shape) for a in small_args] + [_HBM] * 4,
        out_specs=[prompt_rows, per_seq(1), per_seq(CONV_WIDTH - 1)] + [sample_rows] * 4,
        out_shape=[jax.ShapeDtypeStruct((bsz, seq, d), F32),
                   jax.ShapeDtypeStruct((bsz, 1, d), F32),
                   jax.ShapeDtypeStruct((bsz, CONV_WIDTH - 1, d), F32)]
                  + [jax.ShapeDtypeStruct((ms, d), F32)] * 4,
        scratch_shapes=[pltpu.VMEM((d, win_cols), BF16), pltpu.VMEM((d, d), BF16),
                        pltpu.VMEM((d, d), BF16), pltpu.VMEM((d, d), BF16),
                        pltpu.VMEM(stage_in, F32), pltpu.VMEM(stage_sq, F32),
                        pltpu.SemaphoreType.DMA((STAGE_SLOTS,)),
                        pltpu.VMEM((HIST + tm, d), F32),
                        pltpu.VMEM((tm, d), F32),
                        pltpu.VMEM((tm, d), F32),
                        pltpu.VMEM((tm, d), BF16),
                        pltpu.VMEM((1, d), F32)],
        compiler_params=pltpu.CompilerParams(dimension_semantics=("arbitrary",), vmem_limit_bytes=vmem),
    )(xp, xs, h0, cprev, *small_args, win, pa, pb, wo)


def _block_diag_tiles(w):
    w = w.reshape(GATE_BLOCKS, HEADS_PER_BLOCK, B_HEAD_DIM, B_HEAD_DIM)
    eye = jnp.eye(HEADS_PER_BLOCK, dtype=w.dtype)
    t = jnp.einsum('bhij,hk->bhikj', w, eye)
    return t.reshape(GATE_BLOCKS, MXU_DIM, MXU_DIM)


def kernel(x_prompt, x_sample, state_lru_h, state_conv, ffn1_norm, ffn1_w_gate, ffn1_w_up, ffn1_w_down, mix_norm, w_in, gmlp_v_norm, spatial_w, spatial_b, conv_w, conv_b, lru_w_r, lru_b_r, lru_w_i, lru_b_i, lru_lambda, proj_a, proj_b, w_out, ffn2_norm, ffn2_w_gate, ffn2_w_up, ffn2_w_down, final_norm):
    depth = ffn1_norm.shape[0]
    bsz, seq, d = x_prompt.shape
    nsamp = x_sample.shape[0]
    row = lambda p: p.reshape(1, -1)
    fin = row(final_norm)

    xp = x_prompt
    xs = x_sample.reshape(nsamp, d)
    h_p, c_p, h_s, c_s, v_s = [], [], [], [], []
    for l in range(depth):
        small = {
            "mix_norm": row(mix_norm[l]), "v_norm": row(gmlp_v_norm[l]),
            "spatial_w": spatial_w[l], "spatial_b_t": spatial_b[l].T,
            "spatial_w0": row(jnp.repeat(spatial_w[l][:, 0, 0], A_GROUP_DIM)),
            "spatial_b0": row(jnp.repeat(spatial_b[l][:, 0], A_GROUP_DIM)),
            "conv_w": conv_w[l], "conv_b": row(conv_b[l]),
            "w_r": _block_diag_tiles(lru_w_r[l]).astype(BF16), "b_r": row(lru_b_r[l]),
            "w_i": _block_diag_tiles(lru_w_i[l]).astype(BF16), "b_i": row(lru_b_i[l]),
            "lam": row(lru_lambda[l]),
        }
        last = l == depth - 1

        xp2d, xs = _ffn(xp.reshape(bsz * seq, d), xs, row(ffn1_norm[l]), fin,
                        ffn1_w_gate[l].T, ffn1_w_up[l].T, ffn1_w_down[l], final_norm=False)
        cprev = jnp.swapaxes(state_conv[l], 0, 1)
        xp, hl, cn, xs, hn, xb_new, v_new = _mixer(
            xp2d.reshape(bsz, seq, d), xs, state_lru_h[l], cprev, small,
            w_in[l], proj_a[l], proj_b[l], w_out[l])
        xp2d, xs = _ffn(xp.reshape(bsz * seq, d), xs, row(ffn2_norm[l]), fin,
                        ffn2_w_gate[l].T, ffn2_w_up[l].T, ffn2_w_down[l], final_norm=last)
        xp = xp2d.reshape(bsz, seq, d)

        h_p.append(hl.reshape(bsz, d))
        c_p.append(cn)
        h_s.append(hn)
        c_s.append(jnp.concatenate([state_conv[l][:, 1:], xb_new[:, None, :]], axis=1))
        v_s.append(v_new.reshape(nsamp, 1, d))

    return (xp, xs.reshape(nsamp, 1, d), jnp.stack(h_p), jnp.stack(c_p),
            jnp.stack(h_s), jnp.stack(c_s), jnp.stack(v_s))
```
